```python
import math
import jax, jax.numpy as jnp
from jax import lax
import numpy as np

D_MODEL = 4096
BATCH = 4
SEQ = 4096
DEPTH = 1

CTX_LEN = 256
GRID_W = 64
A_WIDTH = 2048
A_HEADS = 16
A_HEAD_K = 128
A_HEAD_V = A_WIDTH // A_HEADS
A_FDIM = A_HEADS * A_HEAD_K
SCAN_CHUNK = 64
B_WIDTH = 2048
B_GROUPS = 16
B_GROUP_CH = B_WIDTH // B_GROUPS
MIX_CHUNK = 128
D_FF = 4 * D_MODEL
N_MOD = 6
ALPHA = (2.0 * DEPTH) ** 0.25
BETA = (8.0 * DEPTH) ** -0.25
LN_EPS = 1e-6
POS_BASE = 10000.0
IN_SIZES = (A_FDIM, A_WIDTH, A_FDIM, A_FDIM, A_WIDTH, B_WIDTH, B_WIDTH, D_MODEL, D_MODEL)
N_IN = sum(IN_SIZES)

kernel_name = "hybrid_hgrn2_chunkmlp_dit_block"


def _layernorm(x, g=None, b=None):
    xf = x.astype(jnp.float32)
    mu = jnp.mean(xf, axis=-1, keepdims=True)
    xc = xf - mu
    y = xc * lax.rsqrt(jnp.mean(xc * xc, axis=-1, keepdims=True) + LN_EPS)
    if g is not None:
        y = y * g.astype(jnp.float32) + b.astype(jnp.float32)
    return y.astype(x.dtype)


def _adaln(x, shift, scale):
    return _layernorm(x) * (1.0 + scale) + shift


def _sincos_2d(rows, cols, dim):
    quarter = dim // 4
    omega = 1.0 / (POS_BASE ** (jnp.arange(quarter, dtype=jnp.float32) / quarter))
    r, cc = jnp.meshgrid(jnp.arange(rows, dtype=jnp.float32), jnp.arange(cols, dtype=jnp.float32), indexing="ij")
    def emb(p):
        a = p.reshape(-1)[:, None] * omega[None, :]
        return jnp.concatenate([jnp.sin(a), jnp.cos(a)], axis=-1)
    return jnp.concatenate([emb(r), emb(cc)], axis=-1)


def _in_proj(h, w_in):
    idx, acc = [], 0
    for s in IN_SIZES[:-1]:
        acc += s
        idx.append(acc)
    return jnp.split(h @ w_in, idx, axis=-1)


def _rec_inputs(q, i, zf, zb, lb_f, lb_b):
    bn, t, _ = q.shape
    def heads(a, d):
        return a.astype(jnp.float32).reshape(bn, t, A_HEADS, d)
    qh = heads(jax.nn.silu(q), A_HEAD_K) * (A_HEAD_K ** -0.5)
    vh = heads(i, A_HEAD_V)
    def gate(z, lb):
        z = z.astype(jnp.float32)
        logf = jnp.log(lb + (1.0 - lb) * jax.nn.sigmoid(z))
        k = (1.0 - lb) * jax.nn.sigmoid(-z)
        return heads(k, A_HEAD_K), heads(logf, A_HEAD_K)
    kf, lff = gate(zf, lb_f)
    kb, lfb = gate(zb, lb_b)
    return qh, vh, kf, lff, kb, lfb


def _chunk_scan(q, k, v, logf, s0):
    bn, t, h, _ = q.shape
    n = t // SCAN_CHUNK
    def to_chunks(a):
        return a.reshape(bn, n, SCAN_CHUNK, h, a.shape[-1]).transpose(1, 0, 3, 2, 4)
    mask = jnp.tril(jnp.ones((SCAN_CHUNK, SCAN_CHUNK), dtype=bool))
    def step(s, inp):
        qc, kc, vc, lc = inp
        b = jnp.cumsum(lc, axis=2)
        diff = b[:, :, :, None, :] - b[:, :, None, :, :]
        decay = jnp.exp(jnp.where(mask[:, :, None], diff, -jnp.inf))
        att = jnp.einsum("bhtsk,bhsk->bhts", qc[:, :, :, None, :] * decay, kc)
        o = jnp.einsum("bhts,bhsv->bhtv", att, vc) + jnp.einsum("bhtk,bhkv->bhtv", qc * jnp.exp(b), s)
        b_last = b[:, :, -1:, :]
        s_new = jnp.exp(b_last[:, :, 0, :])[..., None] * s + jnp.einsum("bhsk,bhsv->bhkv", kc * jnp.exp(b_last - b), vc)
        return s_new, o
    s_fin, o = lax.scan(step, s0, (to_chunks(q), to_chunks(k), to_chunks(v), to_chunks(logf)))
    o = o.transpose(1, 0, 3, 2, 4).reshape(bn, t, h, v.shape[-1])
    return s_fin, o


def _bidir(qh, vh, kf, lff, kb, lfb, s_f, s_b):
    flip = lambda a: jnp.flip(a, axis=1)
    sf, of = _chunk_scan(qh, kf, vh, lff, s_f)
    sb, ob = _chunk_scan(flip(qh), flip(kb), flip(vh), flip(lfb), s_b)
    return sf, sb, of + flip(ob)


def _hgrn2_readout(o, g, gain):
    bn, t = o.shape[0], o.shape[1]
    o = o * lax.rsqrt(jnp.mean(o * o, axis=-1, keepdims=True) + LN_EPS) * gain.astype(jnp.float32)
    return (o.reshape(bn, t, A_WIDTH) * jax.nn.silu(g.astype(jnp.float32))).astype(g.dtype)


def _chunk_mix(u, v, v_g, v_b, w_s, b_s):
    bn, t, _ = v.shape
    n = t // MIX_CHUNK
    vn = _layernorm(v, v_g, v_b).reshape(bn, n, MIX_CHUNK, B_GROUPS, B_GROUP_CH)
    mixed = jnp.einsum("gts,bnsgc->bntgc", w_s, vn) + b_s.T[None, None, :, :, None]
    return u * mixed.reshape(bn, t, B_WIDTH)


def _stream_update(x, p, o_rec, mod, a_norm_g, v_norm_g, v_norm_b, w_s, b_s, w_proj_a, w_proj_b,
                   w_out, ln1_g, ln1_b, w_ff1, w_ff2, ln2_g, ln2_b):
    y_a = _hgrn2_readout(o_rec, p[4], a_norm_g)
    y_b = _chunk_mix(p[5], p[6], v_norm_g, v_norm_b, w_s, b_s)
    merged = jax.nn.sigmoid(p[7]) * (y_a @ w_proj_a) + jax.nn.sigmoid(p[8]) * (y_b @ w_proj_b)
    x = _layernorm(ALPHA * x + mod[2] * (merged @ w_out), ln1_g, ln1_b)
    h = _adaln(x, mod[3], mod[4])
    ff = jnp.square(jax.nn.relu(h @ w_ff1)) @ w_ff2
    return _layernorm(ALPHA * x + mod[5] * ff, ln2_g, ln2_b)


def setup_inputs(seed: int = 0) -> dict:
    key = jax.random.key(seed)
    ks = jax.random.split(key, 24)
    f32 = jnp.float32
    nrm = lambda k, shape, s: jax.random.normal(k, shape, f32) * s
    return {
        "x": nrm(ks[0], (BATCH, SEQ, D_MODEL), 1.0),
        "c": nrm(ks[1], (BATCH, D_MODEL), 1.0),
        "ctx": nrm(ks[2], (BATCH, CTX_LEN, D_MODEL), 1.0),
        "c_ctx": nrm(ks[3], (D_MODEL,), 1.0),
        "w_ada": nrm(ks[4], (DEPTH, D_MODEL, N_MOD * D_MODEL), 0.5 * D_MODEL ** -0.5),
        "b_ada": nrm(ks[5], (DEPTH, N_MOD * D_MODEL), 0.02),
        "w_in": nrm(ks[6], (DEPTH, D_MODEL, N_IN), D_MODEL ** -0.5),
        "lb_logits": nrm(ks[7], (2, DEPTH + 1, A_FDIM), 0.5),
        "a_norm_g": 1.0 + nrm(ks[8], (DEPTH, A_HEAD_V), 0.02),
        "w_proj_a": nrm(ks[9], (DEPTH, A_WIDTH, D_MODEL), BETA * A_WIDTH ** -0.5),
        "v_norm_g": 1.0 + nrm(ks[10], (DEPTH, B_WIDTH), 0.02),
        "v_norm_b": nrm(ks[11], (DEPTH, B_WIDTH), 0.02),
        "w_s": nrm(ks[12], (DEPTH, B_GROUPS, MIX_CHUNK, MIX_CHUNK), MIX_CHUNK ** -0.5),
        "b_s": 1.0 + nrm(ks[13], (DEPTH, B_GROUPS, MIX_CHUNK), 0.02),
        "w_proj_b": nrm(ks[14], (DEPTH, B_WIDTH, D_MODEL), BETA * B_WIDTH ** -0.5),
        "w_out": nrm(ks[15], (DEPTH, D_MODEL, D_MODEL), BETA * D_MODEL ** -0.5),
        "ln1_g": 1.0 + nrm(ks[16], (DEPTH, D_MODEL), 0.02),
        "ln1_b": nrm(ks[17], (DEPTH, D_MODEL), 0.02),
        "w_ff1": nrm(ks[18], (DEPTH, D_MODEL, D_FF), BETA * D_MODEL ** -0.5),
        "w_ff2": nrm(ks[19], (DEPTH, D_FF, D_MODEL), BETA * D_FF ** -0.5),
        "ln2_g": 1.0 + nrm(ks[20], (DEPTH, D_MODEL), 0.02),
        "ln2_b": nrm(ks[21], (DEPTH, D_MODEL), 0.02),
    }


def reference(x, c, ctx, c_ctx, w_ada, b_ada, w_in, lb_logits, a_norm_g, w_proj_a, v_norm_g, v_norm_b,
              w_s, b_s, w_proj_b, w_out, ln1_g, ln1_b, w_ff1, w_ff2, ln2_g, ln2_b):
    bn, t, _ = x.shape
    rows = t // GRID_W
    x = x + _sincos_2d(rows, GRID_W, D_MODEL).astype(x.dtype)[None]
    xc = ctx
    lb_all = jnp.cumsum(jax.nn.softmax(lb_logits.astype(jnp.float32), axis=1), axis=1)
    zero_state = jnp.zeros((bn, A_HEADS, A_HEAD_K, A_HEAD_V), jnp.float32)
    for l in range(DEPTH):
        last = l == DEPTH - 1
        mx = jnp.split((jax.nn.silu(c) @ w_ada[l] + b_ada[l])[:, None, :], N_MOD, axis=-1)
        mc = jnp.split((jax.nn.silu(c_ctx) @ w_ada[l] + b_ada[l])[None, None, :], N_MOD, axis=-1)
        lb_f, lb_b = lb_all[0, l], lb_all[1, l]
        px = _in_proj(_adaln(x, mx[0], mx[1]), w_in[l])
        pc = _in_proj(_adaln(xc, mc[0], mc[1]), w_in[l])
        s_f, s_b, o_c = _bidir(*_rec_inputs(pc[0], pc[1], pc[2], pc[3], lb_f, lb_b), zero_state, zero_state)
        _, _, o_x = _bidir(*_rec_inputs(px[0], px[1], px[2], px[3], lb_f, lb_b), s_f, s_b)
        lw = (a_norm_g[l], v_norm_g[l], v_norm_b[l], w_s[l], b_s[l], w_proj_a[l], w_proj_b[l], w_out[l],
              ln1_g[l], ln1_b[l], w_ff1[l], w_ff2[l], ln2_g[l], ln2_b[l])
        x_new = _stream_update(x, px, o_x, mx, *lw)
        if not last:
            xc = _stream_update(xc, pc, o_c, mc, *lw)
        x = x_new
    return x
```

```python
import functools
import math

import jax
import jax.numpy as jnp
from jax import lax
from jax.experimental import pallas as pl
from jax.experimental.pallas import tpu as pltpu

F32 = jnp.float32
BF16 = jnp.bfloat16

GRID_W = 64
LN_EPS = 1e-6
POS_BASE = 10000.0
N_MOD = 6
HEAD_DIM = 128
SCAN_BLOCK = 128
MXU_TILE = 1024
VMEM_LIMIT = 56 * 1024 * 1024


def _cparams(*sem):
    return pltpu.CompilerParams(dimension_semantics=sem, vmem_limit_bytes=VMEM_LIMIT)


def _sigmoid(x):
    return 1.0 / (1.0 + jnp.exp(-x))


def _ln_rows(x):
    mu = jnp.mean(x, axis=-1, keepdims=True)
    xc = x - mu
    return xc * lax.rsqrt(jnp.mean(xc * xc, axis=-1, keepdims=True) + LN_EPS)


def _mod_kernel(c_ref, w_ref, b_ref, o_ref):
    a = c_ref[...]
    a = a * _sigmoid(a)
    o_ref[...] = jnp.dot(a.astype(BF16), w_ref[...].astype(BF16),
                         preferred_element_type=F32) + b_ref[...]


def _modulation(cc, w_ada, b_ada):
    rows, d = cc.shape
    n = w_ada.shape[1]
    tn = min(512, n)
    return pl.pallas_call(
        _mod_kernel,
        grid=(n // tn,),
        in_specs=[pl.BlockSpec((rows, d), lambda j: (0, 0)),
                  pl.BlockSpec((d, tn), lambda j: (0, j)),
                  pl.BlockSpec((1, tn), lambda j: (0, j))],
        out_specs=pl.BlockSpec((rows, tn), lambda j: (0, j)),
        out_shape=jax.ShapeDtypeStruct((rows, n), F32),
        compiler_params=_cparams("arbitrary"),
        name="modulation",
    )(cc, w_ada, b_ada)


def _adaln_pos_kernel(x_ref, rt_ref, ct_ref, sh_ref, sc_ref, xp_ref, h_ref):
    half = rt_ref.shape[-1]
    d = 2 * half
    xa = x_ref[:, :, :half] + rt_ref[...]
    xb = x_ref[:, :, half:] + ct_ref[...]
    xp_ref[:, :, :half] = xa
    xp_ref[:, :, half:] = xb
    mu = (jnp.sum(xa, axis=-1, keepdims=True) + jnp.sum(xb, axis=-1, keepdims=True)) * (1.0 / d)
    xa = xa - mu
    xb = xb - mu
    var = (jnp.sum(xa * xa, axis=-1, keepdims=True) + jnp.sum(xb * xb, axis=-1, keepdims=True)) * (1.0 / d)
    inv = lax.rsqrt(var + LN_EPS)
    sh = sh_ref[...]
    sc = 1.0 + sc_ref[...]
    h_ref[:, :, :half] = (xa * inv * sc[:, :half] + sh[:, :half]).astype(BF16)
    h_ref[:, :, half:] = (xb * inv * sc[:, half:] + sh[:, half:]).astype(BF16)


def _adaln_pos(x, row_tab, col_tab, mod3, rows_per_step):
    b, t, d = x.shape
    gr = t // GRID_W
    r = rows_per_step
    x4 = x.reshape(b, gr, GRID_W, d)
    blk = pl.BlockSpec((None, r, GRID_W, d), lambda i, j: (i, j, 0, 0))
    xp, h = pl.pallas_call(
        _adaln_pos_kernel,
        grid=(b, gr // r),
        in_specs=[blk,
                  pl.BlockSpec((r, 1, d // 2), lambda i, j: (j, 0, 0)),
                  pl.BlockSpec((1, GRID_W, d // 2), lambda i, j: (0, 0, 0)),
                  pl.BlockSpec((None, 1, d), lambda i, j: (i, 0, 0)),
                  pl.BlockSpec((None, 1, d), lambda i, j: (i, 0, 1))],
        out_specs=[blk, blk],
        out_shape=[jax.ShapeDtypeStruct(x4.shape, F32), jax.ShapeDtypeStruct(x4.shape, BF16)],
        compiler_params=_cparams("parallel", "parallel"),
        name="adaln_pos",
    )(x4, row_tab, col_tab, mod3, mod3)
    return xp.reshape(b * t, d), h.reshape(b * t, d)


def _adaln_rows_kernel(x_ref, sh_ref, sc_ref, h_ref):
    y = _ln_rows(x_ref[...])
    h_ref[...] = (y * (1.0 + sc_ref[...]) + sh_ref[...]).astype(BF16)


def _adaln_ctx(xc, mod3, ctx_row, tm):
    m, d = xc.shape
    return pl.pallas_call(
        _adaln_rows_kernel,
        grid=(m // tm,),
        in_specs=[pl.BlockSpec((tm, d), lambda i: (i, 0)),
                  pl.BlockSpec((None, 1, d), lambda i: (ctx_row, 0, 0)),
                  pl.BlockSpec((None, 1, d), lambda i: (ctx_row, 0, 1))],
        out_specs=pl.BlockSpec((tm, d), lambda i: (i, 0)),
        out_shape=jax.ShapeDtypeStruct((m, d), BF16),
        compiler_params=_cparams("parallel"),
        name="adaln_ctx",
    )(xc, mod3, mod3)


def _ep_silu(scale, acc, o_ref):
    o_ref[...] = (acc * _sigmoid(acc) * scale).astype(o_ref.dtype)


def _ep_cast(acc, o_ref):
    o_ref[...] = acc.astype(o_ref.dtype)


def _ep_sigmoid(acc, o_ref):
    o_ref[...] = _sigmoid(acc).astype(o_ref.dtype)


def _ep_decay(acc, lb_ref, k_ref, lf_ref):
    lb = lb_ref[...]
    e = jnp.exp(-jnp.abs(acc))
    r = 1.0 / (1.0 + e)
    er = e * r
    pos = acc >= 0.0
    s = jnp.where(pos, r, er)
    sn = jnp.where(pos, er, r)
    k_ref[...] = ((1.0 - lb) * sn).astype(k_ref.dtype)
    lf_ref[...] = jnp.log(lb + (1.0 - lb) * s)


def _proj_kernel(epilogue, n_extra, x_ref, w_ref, *refs):
    acc = jnp.dot(x_ref[...], w_ref[...], preferred_element_type=F32)
    epilogue(acc, *refs[:n_extra], *refs[n_extra:])


def _in_proj(h, w, col0, width, epilogue, out_dtypes, extra=(), name="in_proj"):
    m, d = h.shape
    tm = min(MXU_TILE, m)
    tn = math.gcd(col0, width, MXU_TILE)
    off = col0 // tn
    out_blk = pl.BlockSpec((tm, tn), lambda i, j: (i, j))
    return pl.pallas_call(
        functools.partial(_proj_kernel, epilogue, len(extra)),
        grid=(m // tm, width // tn),
        in_specs=[pl.BlockSpec((tm, d), lambda i, j: (i, 0)),
                  pl.BlockSpec((d, tn), lambda i, j: (0, j + off))]
                 + [pl.BlockSpec((1, tn), lambda i, j: (0, j)) for _ in extra],
        out_specs=[out_blk for _ in out_dtypes],
        out_shape=[jax.ShapeDtypeStruct((m, width), dt) for dt in out_dtypes],
        compiler_params=_cparams("parallel", "arbitrary"),
        name=name,
    )(h, w, *extra)


def _sibling(x, m, odd):
    n = x.shape[0]
    return jnp.where(odd, pltpu.roll(x, m, 0), pltpu.roll(x, n - m, 0))


def _scan_block(q, v, kf, kb, lf, lb, with_out):
    c, k = lf.shape
    r = lax.broadcasted_iota(jnp.int32, (c, k), 0)
    cin_f, rex_f, tot_f = lf, jnp.zeros_like(lf), lf
    rin_b, cex_b, tot_b = lb, jnp.zeros_like(lb), lb
    levels = []
    m = 1
    while m < c:
        odd = (r & m) != 0
        if with_out:
            eq = jnp.exp(jnp.where(odd, cin_f, rin_b)).astype(BF16)
            ek = jnp.exp(jnp.where(odd, cex_b, rex_f)).astype(BF16)
            levels.append(pl.dot(q * eq, jnp.where(odd, kb, kf) * ek, trans_b=True))
        sib_f = _sibling(tot_f, m, odd)
        sib_b = _sibling(tot_b, m, odd)
        cin_f = jnp.where(odd, cin_f + sib_f, cin_f)
        rex_f = jnp.where(odd, rex_f, rex_f + sib_f)
        cex_b = jnp.where(odd, cex_b + sib_b, cex_b)
        rin_b = jnp.where(odd, rin_b, rin_b + sib_b)
        tot_f = tot_f + sib_f
        tot_b = tot_b + sib_b
        m *= 2

    ks = jnp.concatenate([kf * jnp.exp(rex_f).astype(BF16), kb * jnp.exp(cex_b).astype(BF16)], axis=1)
    u_t = pl.dot(v, ks, trans_a=True)
    dec = jnp.exp(jnp.concatenate([tot_f[0:1], tot_b[0:1]], axis=1))
    if not with_out:
        return None, None, u_t, dec

    qi = jnp.concatenate([q * jnp.exp(cin_f).astype(BF16), q * jnp.exp(rin_b).astype(BF16)], axis=1)
    x = (lax.broadcasted_iota(jnp.int32, (c, c), 0) ^ lax.broadcasted_iota(jnp.int32, (c, c), 1))
    att = levels[-1]
    m = c // 2
    for a in reversed(levels[:-1]):
        att = jnp.where(x < m, a, att)
        m //= 2
    att = jnp.where(x < 1, pl.dot(q, kf + kb, trans_b=True), att)
    o = jnp.dot(att.astype(BF16), v, preferred_element_type=F32)
    return o, qi, u_t, dec


def _scan_kernel(nx, nc,
                 qx, vx, kfx, kbx, lfx, lbx, gx, qc, vc, kfc, kbc, lfc, lbc, gain_ref,
                 y_ref, oi_ref, qi_ref, u_ref, d_ref, s_ref):
    c = SCAN_BLOCK
    k = HEAD_DIM

    for j in range(nc):
        rows = pl.ds(j * c, c)
        _, _, u_t, dec = _scan_block(qc[rows, :], vc[rows, :], kfc[rows, :], kbc[rows, :],
                                     lfc[rows, :], lbc[rows, :], False)
        u_ref[j] = u_t
        d_ref[j] = dec

    def block_body(n, carry):
        rows = pl.ds(pl.multiple_of(n * c, c), c)
        o, qi, u_t, dec = _scan_block(qx[rows, :], vx[rows, :], kfx[rows, :], kbx[rows, :],
                                      lfx[rows, :], lbx[rows, :], True)
        oi_ref[rows, :] = o
        qi_ref[rows, :] = qi
        u_ref[nc + n] = u_t
        d_ref[nc + n] = dec
        return carry

    lax.fori_loop(0, nx, block_body, 0)

    sf = jnp.zeros((k, k), F32)
    sb = jnp.zeros((k, k), F32)
    for j in range(nc):
        sf = d_ref[j][:, :k] * sf + u_ref[j][:, :k]
        jb = nc - 1 - j
        sb = d_ref[jb][:, k:] * sb + u_ref[jb][:, k:]

    def sweep_body(i, carry):
        sf, sb = carry
        s_ref[i, :, :k] = sf.astype(BF16)
        sf = d_ref[nc + i][:, :k] * sf + u_ref[nc + i][:, :k]
        j = nx - 1 - i
        s_ref[j, :, k:] = sb.astype(BF16)
        sb = d_ref[nc + j][:, k:] * sb + u_ref[nc + j][:, k:]
        return sf, sb

    lax.fori_loop(0, nx, sweep_body, (sf, sb))

    gain = gain_ref[...]

    def out_body(n, carry):
        rows = pl.ds(pl.multiple_of(n * c, c), c)
        o = oi_ref[rows, :] + pl.dot(qi_ref[rows, :], s_ref[n], trans_b=True)
        o = o * lax.rsqrt(jnp.mean(o * o, axis=-1, keepdims=True) + LN_EPS) * gain
        y_ref[rows, :] = (o * gx[rows, :].astype(F32)).astype(BF16)
        return carry

    lax.fori_loop(0, nx, out_body, 0)


def _hgrn2_scan(px, pc, gain, b, t, tc):
    qx, vx, kkx, lfx, gx = px
    qc, vc, kkc, lfc = pc
    k = HEAD_DIM
    heads = qx.shape[1] // k
    nx, nc = t // SCAN_BLOCK, tc // SCAN_BLOCK
    fwd = lambda rows: pl.BlockSpec((rows, k), lambda i, h: (i, h))
    bwd = lambda rows: pl.BlockSpec((rows, k), lambda i, h: (i, h + heads))
    return pl.pallas_call(
        functools.partial(_scan_kernel, nx, nc),
        grid=(b, heads),
        in_specs=[fwd(t), fwd(t), fwd(t), bwd(t), fwd(t), bwd(t), fwd(t),
                  fwd(tc), fwd(tc), fwd(tc), bwd(tc), fwd(tc), bwd(tc),
                  pl.BlockSpec((1, k), lambda i, h: (0, 0))],
        out_specs=fwd(t),
        out_shape=jax.ShapeDtypeStruct((b * t, heads * k), BF16),
        scratch_shapes=[pltpu.VMEM((t, k), F32),
                        pltpu.VMEM((t, 2 * k), BF16),
                        pltpu.VMEM((nx + nc, k, 2 * k), F32),
                        pltpu.VMEM((nx + nc, 1, 2 * k), F32),
                        pltpu.VMEM((nx, k, 2 * k), BF16)],
        compiler_params=_cparams("parallel", "parallel"),
        name="hgrn2_scan",
    )(qx, vx, kkx, kkx, lfx, lfx, gx, qc, vc, kkc, kkc, lfc, lfc, gain)


def _mix_kernel(n_groups, chunk, u_ref, v_ref, g_ref, b_ref, ws_ref, bias_ref, y_ref, vn_ref):
    vn_ref[...] = (_ln_rows(v_ref[...].astype(F32)) * g_ref[...] + b_ref[...]).astype(BF16)
    gc = v_ref.shape[1] // n_groups
    for cidx in range(v_ref.shape[0] // chunk):
        rows = pl.ds(cidx * chunk, chunk)
        for g in range(n_groups):
            cols = pl.ds(g * gc, gc)
            mixed = jnp.dot(ws_ref[g], vn_ref[rows, cols], preferred_element_type=F32)
            y_ref[rows, cols] = (u_ref[rows, cols].astype(F32) * (mixed + bias_ref[:, cols])).astype(BF16)


def _chunk_mix(uv, v_g, v_b, w_s, bias, rows_per_step):
    m = uv.shape[0]
    bw = uv.shape[1] // 2
    groups, chunk, _ = w_s.shape
    tm = rows_per_step
    return pl.pallas_call(
        functools.partial(_mix_kernel, groups, chunk),
        grid=(m // tm,),
        in_specs=[pl.BlockSpec((tm, bw), lambda i: (i, 0)),
                  pl.BlockSpec((tm, bw), lambda i: (i, 1)),
                  pl.BlockSpec((1, bw), lambda i: (0, 0)),
                  pl.BlockSpec((1, bw), lambda i: (0, 0)),
                  pl.BlockSpec((groups, chunk, chunk), lambda i: (0, 0, 0)),
                  pl.BlockSpec((chunk, bw), lambda i: (0, 0))],
        out_specs=pl.BlockSpec((tm, bw), lambda i: (i, 0)),
        out_shape=jax.ShapeDtypeStruct((m, bw), BF16),
        scratch_shapes=[pltpu.VMEM((tm, bw), BF16)],
        compiler_params=_cparams("parallel"),
        name="chunk_mix",
    )(uv, uv, v_g, v_b, w_s, bias)


def _merge_kernel(ya_ref, yb_ref, wa_ref, wb_ref, ga_ref, gb_ref, o_ref):
    a = jnp.dot(ya_ref[...], wa_ref[...], preferred_element_type=F32)
    bq = jnp.dot(yb_ref[...], wb_ref[...], preferred_element_type=F32)
    o_ref[...] = (ga_ref[...].astype(F32) * a + gb_ref[...].astype(F32) * bq).astype(BF16)


def _merge(ya, yb, wa, wb, gates):
    m, ka = ya.shape
    kb = yb.shape[1]
    d = wa.shape[1]
    tm, tn = min(MXU_TILE, m), min(MXU_TILE, d)
    nd = d // tn
    return pl.pallas_call(
        _merge_kernel,
        grid=(m // tm, nd),
        in_specs=[pl.BlockSpec((tm, ka), lambda i, j: (i, 0)),
                  pl.BlockSpec((tm, kb), lambda i, j: (i, 0)),
                  pl.BlockSpec((ka, tn), lambda i, j: (0, j)),
                  pl.BlockSpec((kb, tn), lambda i, j: (0, j)),
                  pl.BlockSpec((tm, tn), lambda i, j: (i, j)),
                  pl.BlockSpec((tm, tn), lambda i, j: (i, j + nd))],
        out_specs=pl.BlockSpec((tm, tn), lambda i, j: (i, j)),
        out_shape=jax.ShapeDtypeStruct((m, d), BF16),
        compiler_params=_cparams("parallel", "arbitrary"),
        name="branch_merge",
    )(ya, yb, wa, wb, gates, gates)


def _resid_kernel(alpha, x_ref, w_ref, r_ref, gate_ref, o_ref):
    y = jnp.dot(x_ref[...], w_ref[...], preferred_element_type=F32)
    o_ref[...] = alpha * r_ref[...] + gate_ref[...] * y


def _out_proj_resid(xin, w, resid, mod3, gate_idx, alpha, t):
    m, kdim = xin.shape
    d = w.shape[1]
    tm, tn = min(MXU_TILE, m, t), min(MXU_TILE, d)
    per_batch = t // tm
    nd = d // tn
    return pl.pallas_call(
        functools.partial(_resid_kernel, alpha),
        grid=(m // tm, nd),
        in_specs=[pl.BlockSpec((tm, kdim), lambda i, j: (i, 0)),
                  pl.BlockSpec((kdim, tn), lambda i, j: (0, j)),
                  pl.BlockSpec((tm, tn), lambda i, j: (i, j)),
                  pl.BlockSpec((None, 1, tn), lambda i, j: (i // per_batch, 0, gate_idx * nd + j))],
        out_specs=pl.BlockSpec((tm, tn), lambda i, j: (i, j)),
        out_shape=jax.ShapeDtypeStruct((m, d), F32),
        compiler_params=_cparams("parallel", "arbitrary"),
        name="out_proj_resid",
    )(xin, w, resid, mod3)


def _ln_adaln_kernel(p_ref, g_ref, b_ref, sh_ref, sc_ref, x_ref, h_ref):
    x = _ln_rows(p_ref[...]) * g_ref[...] + b_ref[...]
    x_ref[...] = x
    h_ref[...] = (_ln_rows(x) * (1.0 + sc_ref[...]) + sh_ref[...]).astype(BF16)


def _ln_adaln(pre, g, bvec, mod3, shift_idx, scale_idx, t, tm):
    m, d = pre.shape
    per_batch = t // tm
    row = pl.BlockSpec((tm, d), lambda i: (i, 0))
    vec = pl.BlockSpec((1, d), lambda i: (0, 0))
    return pl.pallas_call(
        _ln_adaln_kernel,
        grid=(m // tm,),
        in_specs=[row, vec, vec,
                  pl.BlockSpec((None, 1, d), lambda i: (i // per_batch, 0, shift_idx)),
                  pl.BlockSpec((None, 1, d), lambda i: (i // per_batch, 0, scale_idx))],
        out_specs=[row, row],
        out_shape=[jax.ShapeDtypeStruct((m, d), F32), jax.ShapeDtypeStruct((m, d), BF16)],
        compiler_params=_cparams("parallel"),
        name="ln_adaln",
    )(pre, g, bvec, mod3, mod3)


def _ln_kernel(p_ref, g_ref, b_ref, o_ref):
    o_ref[...] = _ln_rows(p_ref[...]) * g_ref[...] + b_ref[...]


def _ln(pre, g, bvec, tm):
    m, d = pre.shape
    row = pl.BlockSpec((tm, d), lambda i: (i, 0))
    vec = pl.BlockSpec((1, d), lambda i: (0, 0))
    return pl.pallas_call(
        _ln_kernel,
        grid=(m // tm,),
        in_specs=[row, vec, vec],
        out_specs=row,
        out_shape=jax.ShapeDtypeStruct((m, d), F32),
        compiler_params=_cparams("parallel"),
        name="ln_out",
    )(pre, g, bvec)


def _ffn1_kernel(x_ref, w_ref, o_ref):
    y = jnp.dot(x_ref[...], w_ref[...], preferred_element_type=F32)
    o_ref[...] = jnp.square(jnp.maximum(y, 0.0)).astype(BF16)


def _ffn1(h, w):
    m, d = h.shape
    n = w.shape[1]
    tm, tn = min(MXU_TILE, m), min(MXU_TILE, n)
    return pl.pallas_call(
        _ffn1_kernel,
        grid=(m // tm, n // tn),
        in_specs=[pl.BlockSpec((tm, d), lambda i, j: (i, 0)),
                  pl.BlockSpec((d, tn), lambda i, j: (0, j))],
        out_specs=pl.BlockSpec((tm, tn), lambda i, j: (i, j)),
        out_shape=jax.ShapeDtypeStruct((m, n), BF16),
        compiler_params=_cparams("parallel", "arbitrary"),
        name="ffn_up",
    )(h, w)


def _ffn2_kernel(alpha, a_ref, w_ref, r_ref, gate_ref, o_ref, acc_ref):
    kk = pl.program_id(2)
    y = jnp.dot(a_ref[...], w_ref[...], preferred_element_type=F32)

    @pl.when(kk == 0)
    def _():
        acc_ref[...] = y

    @pl.when(kk > 0)
    def _():
        acc_ref[...] += y

    @pl.when(kk == pl.num_programs(2) - 1)
    def _():
        o_ref[...] = alpha * r_ref[...] + gate_ref[...] * acc_ref[...]


def _ffn2_resid(a, w, resid, mod3, gate_idx, alpha, t):
    m, f = a.shape
    d = w.shape[1]
    tm, tn, tk = min(MXU_TILE, m, t), min(MXU_TILE, d), min(2 * MXU_TILE, f)
    per_batch = t // tm
    nd = d // tn
    return pl.pallas_call(
        functools.partial(_ffn2_kernel, alpha),
        grid=(m // tm, nd, f // tk),
        in_specs=[pl.BlockSpec((tm, tk), lambda i, j, kk: (i, kk)),
                  pl.BlockSpec((tk, tn), lambda i, j, kk: (kk, j)),
                  pl.BlockSpec((tm, tn), lambda i, j, kk: (i, j)),
                  pl.BlockSpec((None, 1, tn), lambda i, j, kk: (i // per_batch, 0, gate_idx * nd + j))],
        out_specs=pl.BlockSpec((tm, tn), lambda i, j, kk: (i, j)),
        out_shape=jax.ShapeDtypeStruct((m, d), F32),
        scratch_shapes=[pltpu.VMEM((tm, tn), F32)],
        compiler_params=_cparams("parallel", "parallel", "arbitrary"),
        name="ffn_down_resid",
    )(a, w, resid, mod3)


def _pos_tables(rows, cols, dim):
    quarter = dim // 4
    omega = 1.0 / (POS_BASE ** (jnp.arange(quarter, dtype=F32) / quarter))

    def emb(p):
        a = p[:, None] * omega[None, :]
        return jnp.concatenate([jnp.sin(a), jnp.cos(a)], axis=-1)

    return (emb(jnp.arange(rows, dtype=F32))[:, None, :], emb(jnp.arange(cols, dtype=F32))[None, :, :])


def kernel(x, c, ctx, c_ctx, w_ada, b_ada, w_in, lb_logits, a_norm_g, w_proj_a, v_norm_g, v_norm_b,
           w_s, b_s, w_proj_b, w_out, ln1_g, ln1_b, w_ff1, w_ff2, ln2_g, ln2_b):
    depth = w_ada.shape[0]
    assert depth == 1, "only the single-layer block is implemented"
    b, t, d = x.shape
    tc = ctx.shape[1]
    a_fdim = lb_logits.shape[-1]
    a_width = w_proj_a.shape[1]
    b_width = w_proj_b.shape[1]
    assert a_fdim == a_width and a_norm_g.shape[-1] == HEAD_DIM
    assert t % GRID_W == 0 and t % SCAN_BLOCK == 0 and tc % SCAN_BLOCK == 0
    assert w_s.shape[2] == SCAN_BLOCK or t % w_s.shape[2] == 0
    alpha = (2.0 * depth) ** 0.25
    m = b * t

    sizes = (a_fdim, a_width, a_fdim, a_fdim, a_width, b_width, b_width, d, d)
    offs = [0]
    for s in sizes:
        offs.append(offs[-1] + s)
    assert offs[-1] == w_in.shape[-1]

    w_in_b = w_in[0].astype(BF16)
    wa_b, wb_b = w_proj_a[0].astype(BF16), w_proj_b[0].astype(BF16)
    wo_b = w_out[0].astype(BF16)
    w1_b, w2_b = w_ff1[0].astype(BF16), w_ff2[0].astype(BF16)
    ws_b = w_s[0].astype(BF16)
    lb_all = jnp.cumsum(jax.nn.softmax(lb_logits.astype(F32), axis=1), axis=1)
    lb_row = jnp.concatenate([lb_all[0, 0], lb_all[1, 0]])[None, :]
    mix_bias = jnp.repeat(b_s[0].T, b_width // w_s.shape[1], axis=1)
    row_tab, col_tab = _pos_tables(t // GRID_W, GRID_W, d)

    n_rows = -(-(b + 1) // 8) * 8
    cc = jnp.concatenate([c, c_ctx[None, :], jnp.zeros((n_rows - b - 1, d), F32)], axis=0)
    mod = _modulation(cc, w_ada[0], b_ada[0][None, :])
    mod3 = mod.reshape(n_rows, 1, N_MOD * d)

    xp, h = _adaln_pos(x, row_tab, col_tab, mod3, rows_per_step=min(4, t // GRID_W))
    hc = _adaln_ctx(ctx.reshape(b * tc, d), mod3, b, tm=min(256, b * tc))

    scale = HEAD_DIM ** -0.5
    ep_q = functools.partial(_ep_silu, scale)
    ep_g = functools.partial(_ep_silu, 1.0)

    def branch_a_inputs(hh, tag):
        (q,) = _in_proj(hh, w_in_b, offs[0], sizes[0], ep_q, [BF16], name="in_proj_q" + tag)
        (v,) = _in_proj(hh, w_in_b, offs[1], sizes[1], _ep_cast, [BF16], name="in_proj_i" + tag)
        kk, lf = _in_proj(hh, w_in_b, offs[2], sizes[2] + sizes[3], _ep_decay, [BF16, F32],
                          extra=(lb_row,), name="in_proj_f" + tag)
        return q, v, kk, lf

    qx, vx, kkx, lfx = branch_a_inputs(h, "")
    qc, vc, kkc, lfc = branch_a_inputs(hc, "_ctx")
    (gx,) = _in_proj(h, w_in_b, offs[4], sizes[4], ep_g, [BF16], name="in_proj_g")
    (uv,) = _in_proj(h, w_in_b, offs[5], sizes[5] + sizes[6], _ep_cast, [BF16], name="in_proj_uv")
    (gates,) = _in_proj(h, w_in_b, offs[7], sizes[7] + sizes[8], _ep_sigmoid, [BF16], name="in_proj_gates")

    ya = _hgrn2_scan((qx, vx, kkx, lfx, gx), (qc, vc, kkc, lfc), a_norm_g[0][None, :], b, t, tc)
    yb = _chunk_mix(uv, v_norm_g[0][None, :], v_norm_b[0][None, :], ws_b, mix_bias,
                    rows_per_step=min(2 * w_s.shape[2], t))

    merged = _merge(ya, yb, wa_b, wb_b, gates)
    pre1 = _out_proj_resid(merged, wo_b, xp, mod3, 2, alpha, t)
    x1, h2 = _ln_adaln(pre1, ln1_g[0][None, :], ln1_b[0][None, :], mod3, 3, 4, t, tm=min(256, t))
    act = _ffn1(h2, w1_b)
    pre2 = _ffn2_resid(act, w2_b, x1, mod3, 5, alpha, t)
    out = _ln(pre2, ln2_g[0][None, :], ln2_b[0][None, :], tm=min(256, m))
    return out.reshape(b, t, d)
```

```python
import functools
import math

import jax
import jax.numpy as jnp
from jax import lax
from jax.experimental import pallas as pl
from jax.experimental.pallas import tpu as pltpu

F32 = jnp.float32
BF16 = jnp.bfloat16

GRID_W = 64
LN_EPS = 1e-6
POS_BASE = 10000.0
N_MOD = 6
HEAD_DIM = 128
SCAN_BLOCK = 128
MXU_TILE = 1024
VMEM_LIMIT = 56 * 1024 * 1024


def _cparams(*sem):
    return pltpu.CompilerParams(dimension_semantics=sem, vmem_limit_bytes=VMEM_LIMIT)


def _sigmoid(x):
    return 0.5 * jnp.tanh(0.5 * x) + 0.5


def _ln_rows(x):
    mu = jnp.mean(x, axis=-1, keepdims=True)
    xc = x - mu
    return xc * lax.rsqrt(jnp.mean(xc * xc, axis=-1, keepdims=True) + LN_EPS)


def _mod_kernel(c_ref, w_ref, b_ref, o_ref):
    a = c_ref[...]
    a = a * _sigmoid(a)
    o_ref[...] = jnp.dot(a.astype(BF16), w_ref[...].astype(BF16),
                         preferred_element_type=F32) + b_ref[...]


def _modulation(cc, w_ada, b_ada):
    rows, d = cc.shape
    n = w_ada.shape[1]
    tn = min(512, n)
    return pl.pallas_call(
        _mod_kernel,
        grid=(n // tn,),
        in_specs=[pl.BlockSpec((rows, d), lambda j: (0, 0)),
                  pl.BlockSpec((d, tn), lambda j: (0, j)),
                  pl.BlockSpec((1, tn), lambda j: (0, j))],
        out_specs=pl.BlockSpec((rows, tn), lambda j: (0, j)),
        out_shape=jax.ShapeDtypeStruct((rows, n), F32),
        compiler_params=_cparams("arbitrary"),
        name="modulation",
    )(cc, w_ada, b_ada)


def _adaln_pos_kernel(x_ref, rt_ref, ct_ref, sh_ref, sc_ref, xp_ref, h_ref):
    half = rt_ref.shape[-1]
    d = 2 * half
    xa = x_ref[:, :, :half] + rt_ref[...]
    xb = x_ref[:, :, half:] + ct_ref[...]
    xp_ref[:, :, :half] = xa
    xp_ref[:, :, half:] = xb
    mu = (jnp.sum(xa, axis=-1, keepdims=True) + jnp.sum(xb, axis=-1, keepdims=True)) * (1.0 / d)
    xa = xa - mu
    xb = xb - mu
    var = (jnp.sum(xa * xa, axis=-1, keepdims=True) + jnp.sum(xb * xb, axis=-1, keepdims=True)) * (1.0 / d)
    inv = lax.rsqrt(var + LN_EPS)
    sh = sh_ref[...]
    sc = 1.0 + sc_ref[...]
    h_ref[:, :, :half] = (xa * inv * sc[:, :half] + sh[:, :half]).astype(BF16)
    h_ref[:, :, half:] = (xb * inv * sc[:, half:] + sh[:, half:]).astype(BF16)


def _adaln_pos(x, row_tab, col_tab, mod3, rows_per_step):
    b, t, d = x.shape
    gr = t // GRID_W
    r = rows_per_step
    x4 = x.reshape(b, gr, GRID_W, d)
    blk = pl.BlockSpec((None, r, GRID_W, d), lambda i, j: (i, j, 0, 0))
    xp, h = pl.pallas_call(
        _adaln_pos_kernel,
        grid=(b, gr // r),
        in_specs=[blk,
                  pl.BlockSpec((r, 1, d // 2), lambda i, j: (j, 0, 0)),
                  pl.BlockSpec((1, GRID_W, d // 2), lambda i, j: (0, 0, 0)),
                  pl.BlockSpec((None, 1, d), lambda i, j: (i, 0, 0)),
                  pl.BlockSpec((None, 1, d), lambda i, j: (i, 0, 1))],
        out_specs=[blk, blk],
        out_shape=[jax.ShapeDtypeStruct(x4.shape, F32), jax.ShapeDtypeStruct(x4.shape, BF16)],
        compiler_params=_cparams("parallel", "parallel"),
        name="adaln_pos",
    )(x4, row_tab, col_tab, mod3, mod3)
    return xp.reshape(b * t, d), h.reshape(b * t, d)


def _adaln_rows_kernel(x_ref, sh_ref, sc_ref, h_ref):
    y = _ln_rows(x_ref[...])
    h_ref[...] = (y * (1.0 + sc_ref[...]) + sh_ref[...]).astype(BF16)


def _adaln_ctx(xc, mod3, ctx_row, tm):
    m, d = xc.shape
    return pl.pallas_call(
        _adaln_rows_kernel,
        grid=(m // tm,),
        in_specs=[pl.BlockSpec((tm, d), lambda i: (i, 0)),
                  pl.BlockSpec((None, 1, d), lambda i: (ctx_row, 0, 0)),
                  pl.BlockSpec((None, 1, d), lambda i: (ctx_row, 0, 1))],
        out_specs=pl.BlockSpec((tm, d), lambda i: (i, 0)),
        out_shape=jax.ShapeDtypeStruct((m, d), BF16),
        compiler_params=_cparams("parallel"),
        name="adaln_ctx",
    )(xc, mod3, mod3)


def _ep_silu(scale, acc, o_ref):
    o_ref[...] = (acc * _sigmoid(acc) * scale).astype(o_ref.dtype)


def _ep_cast(acc, o_ref):
    o_ref[...] = acc.astype(o_ref.dtype)


def _ep_sigmoid(acc, o_ref):
    o_ref[...] = _sigmoid(acc).astype(o_ref.dtype)


def _ep_decay(acc, lb_ref, k_ref, lf_ref):
    lb = lb_ref[...]
    th = 0.5 * jnp.tanh(0.5 * acc)
    one_lb = 1.0 - lb
    k_ref[...] = (one_lb * (0.5 - th)).astype(k_ref.dtype)
    lf_ref[...] = jnp.log2(lb + one_lb * (0.5 + th))


def _proj_kernel(epilogue, n_extra, x_ref, w_ref, *refs):
    acc = jnp.dot(x_ref[...], w_ref[...], preferred_element_type=F32)
    epilogue(acc, *refs[:n_extra], *refs[n_extra:])


def _in_proj(h, w, col0, width, epilogue, out_dtypes, extra=(), name="in_proj"):
    m, d = h.shape
    tm = min(MXU_TILE, m)
    tn = math.gcd(col0, width, MXU_TILE)
    off = col0 // tn
    out_blk = pl.BlockSpec((tm, tn), lambda i, j: (i, j))
    return pl.pallas_call(
        functools.partial(_proj_kernel, epilogue, len(extra)),
        grid=(m // tm, width // tn),
        in_specs=[pl.BlockSpec((tm, d), lambda i, j: (i, 0)),
                  pl.BlockSpec((d, tn), lambda i, j: (0, j + off))]
                 + [pl.BlockSpec((1, tn), lambda i, j: (0, j)) for _ in extra],
        out_specs=[out_blk for _ in out_dtypes],
        out_shape=[jax.ShapeDtypeStruct((m, width), dt) for dt in out_dtypes],
        compiler_params=_cparams("parallel", "arbitrary"),
        name=name,
    )(h, w, *extra)


def _mid_rows(b, m):
    c, w = b.shape
    assert m in (1, 2, 4)
    if m == 1:
        odd = (lax.broadcasted_iota(jnp.int32, (c, w), 0) & 1) == 1
        return jnp.where(odd, pltpu.roll(b, 1, 0), b)
    b3 = b.reshape(c // 8, 8, w)
    if m == 4:
        return jnp.broadcast_to(b3[:, 3:4, :], b3.shape).reshape(c, w)
    low = lax.broadcasted_iota(jnp.int32, b3.shape, 1) < 4
    return jnp.where(low, jnp.broadcast_to(b3[:, 1:2, :], b3.shape),
                     jnp.broadcast_to(b3[:, 5:6, :], b3.shape)).reshape(c, w)


def _block_sums(tri, lf, lb):
    k = lf.shape[1]
    parts = []
    for a in (lf, lb):
        hi = a.astype(BF16)
        parts += [hi, (a - hi.astype(F32)).astype(BF16)]
    cs = jnp.dot(tri, jnp.concatenate(parts, axis=1), preferred_element_type=F32)
    return jnp.concatenate([cs[:, :k] + cs[:, k:2 * k], cs[:, 2 * k:3 * k] + cs[:, 3 * k:]], axis=1)


def _state_terms(v, kf, kb, bcat, lb):
    c, k = lb.shape
    tot = bcat[c - 1:c]
    ks = jnp.concatenate([kf * jnp.exp2(tot[:, :k] - bcat[:, :k]),
                          kb * jnp.exp2(bcat[:, k:] - lb)], axis=1)
    return pl.dot(v, ks.astype(BF16), trans_a=True), jnp.exp2(tot)


def _level_products(q, kf, kb, bcat, lb, par_ref):
    c, k = lb.shape
    xf, xb = bcat[:, :k], bcat[:, k:] - lb
    prods = []
    m, level = 1, 0
    while m < c:
        if m % 8 == 0:
            split = lambda a: a.reshape(c // (2 * m), 2, m, a.shape[-1])
            join = lambda even, odd: jnp.concatenate([even, odd], axis=1).reshape(c, k)
            xf4, xb4, kf4, kb4 = split(xf), split(xb), split(kf), split(kb)
            mid = split(bcat)[:, 0:1, m - 1:m, :]
            mf, mb = mid[..., :k], mid[..., k:]
            eq = join(mb - xb4[:, 0:1], xf4[:, 1:2] - mf)
            ek = join(mf - xf4[:, 0:1], xb4[:, 1:2] - mb)
            ksel = join(kf4[:, 0:1], kb4[:, 1:2])
        else:
            mid = _mid_rows(bcat, m)
            df = xf - mid[:, :k]
            db = mid[:, k:] - xb
            eq = jnp.minimum(df, db)
            ek = -jnp.maximum(df, db)
            ksel = jnp.where(par_ref[level] > 0, kb, kf)
        prods.append(pl.dot(q * jnp.exp2(eq), ksel * jnp.exp2(ek), trans_b=True))
        m *= 2
        level += 1
    prods.append(pl.dot(q, kf + kb, trans_b=True))
    return prods


def _assemble(prods):
    c = prods[0].shape[0]
    x = (lax.broadcasted_iota(jnp.int32, (c, c), 0) ^ lax.broadcasted_iota(jnp.int32, (c, c), 1))
    att = prods[-2]
    m = c // 2
    for a in reversed(prods[:-2]):
        att = jnp.where(x < m, a, att)
        m //= 2
    return jnp.where(x < 1, prods[-1], att).astype(BF16)


def _entry_factors(q, bcat, lb):
    c, k = lb.shape
    xb = bcat[:, k:] - lb
    return jnp.concatenate([q * jnp.exp2(bcat[:, :k]),
                            q * jnp.exp2(bcat[c - 1:c, k:] - xb)], axis=1).astype(BF16)


def _scan_kernel(nx, nc,
                 qx, vx, kfx, kbx, lfx, lbx, gx, qc, vc, kfc, kbc, lfc, lbc, gain_ref, tri_ref, par_ref,
                 y_ref, oi_ref, qi_ref, u_ref, d_ref, s_ref, b_ref):
    c = SCAN_BLOCK
    k = HEAD_DIM
    f32 = lambda ref, rows: ref[rows, :].astype(F32)

    for j in range(nc):
        rows = pl.ds(j * c, c)
        bcat = _block_sums(tri_ref[...], lfc[rows, :], lbc[rows, :])
        u_t, dec = _state_terms(vc[rows, :], f32(kfc, rows), f32(kbc, rows), bcat, lbc[rows, :])
        u_ref[j] = u_t
        d_ref[j] = dec

    def sums_body(n, carry):
        rows = pl.ds(pl.multiple_of(n * c, c), c)
        b_ref[rows, :] = _block_sums(tri_ref[...], lfx[rows, :], lbx[rows, :])
        return carry

    lax.fori_loop(0, nx, sums_body, 0, unroll=math.gcd(nx, 4))

    group = math.gcd(nx, 4)

    def group_body(p, carry):
        rows = [pl.ds(pl.multiple_of((p * group + i) * c, c), c) for i in range(group)]
        qkk = [(f32(qx, rw), f32(kfx, rw), f32(kbx, rw)) for rw in rows]
        prods = [_level_products(q, kf, kb, b_ref[rw, :], lbx[rw, :], par_ref)
                 for (q, kf, kb), rw in zip(qkk, rows)]
        for i, ((_, kf, kb), rw) in enumerate(zip(qkk, rows)):
            u_t, dec = _state_terms(vx[rw, :], kf, kb, b_ref[rw, :], lbx[rw, :])
            u_ref[nc + p * group + i] = u_t
            d_ref[nc + p * group + i] = dec
        outs = [jnp.dot(_assemble(pr), vx[rw, :], preferred_element_type=F32) for pr, rw in zip(prods, rows)]
        for (q, _, _), rw in zip(qkk, rows):
            qi_ref[rw, :] = _entry_factors(q, b_ref[rw, :], lbx[rw, :])
        for o, rw in zip(outs, rows):
            oi_ref[rw, :] = o
        return carry

    lax.fori_loop(0, nx // group, group_body, 0)

    sf = jnp.zeros((k, k), F32)
    sb = jnp.zeros((k, k), F32)
    for j in range(nc):
        sf = d_ref[j][:, :k] * sf + u_ref[j][:, :k]
        jb = nc - 1 - j
        sb = d_ref[jb][:, k:] * sb + u_ref[jb][:, k:]

    def sweep_body(i, carry):
        sf, sb = carry
        s_ref[i, :, :k] = sf.astype(BF16)
        sf = d_ref[nc + i][:, :k] * sf + u_ref[nc + i][:, :k]
        j = nx - 1 - i
        s_ref[j, :, k:] = sb.astype(BF16)
        sb = d_ref[nc + j][:, k:] * sb + u_ref[nc + j][:, k:]
        return sf, sb

    lax.fori_loop(0, nx, sweep_body, (sf, sb))

    gain = gain_ref[...]

    def out_body(n, carry):
        rows = pl.ds(pl.multiple_of(n * c, c), c)
        o = oi_ref[rows, :] + pl.dot(qi_ref[rows, :], s_ref[n], trans_b=True)
        o = o * lax.rsqrt(jnp.mean(o * o, axis=-1, keepdims=True) + LN_EPS) * gain
        y_ref[rows, :] = (o * gx[rows, :].astype(F32)).astype(BF16)
        return carry

    lax.fori_loop(0, nx, out_body, 0, unroll=math.gcd(nx, 4))


def _hgrn2_scan(px, pc, gain, b, t, tc):
    qx, vx, kkx, lfx, gx = px
    qc, vc, kkc, lfc = pc
    k = HEAD_DIM
    heads = qx.shape[1] // k
    nx, nc = t // SCAN_BLOCK, tc // SCAN_BLOCK
    n_levels = SCAN_BLOCK.bit_length() - 1
    bits = (jnp.arange(SCAN_BLOCK)[None, :] >> jnp.arange(n_levels)[:, None]) & 1
    parity = jnp.broadcast_to(bits[:, :, None], (n_levels, SCAN_BLOCK, k)).astype(F32)
    fwd = lambda rows: pl.BlockSpec((rows, k), lambda i, h: (i, h))
    bwd = lambda rows: pl.BlockSpec((rows, k), lambda i, h: (i, h + heads))
    return pl.pallas_call(
        functools.partial(_scan_kernel, nx, nc),
        grid=(b, heads),
        in_specs=[fwd(t), fwd(t), fwd(t), bwd(t), fwd(t), bwd(t), fwd(t),
                  fwd(tc), fwd(tc), fwd(tc), bwd(tc), fwd(tc), bwd(tc),
                  pl.BlockSpec((1, k), lambda i, h: (0, 0)),
                  pl.BlockSpec((SCAN_BLOCK, SCAN_BLOCK), lambda i, h: (0, 0)),
                  pl.BlockSpec(parity.shape, lambda i, h: (0, 0, 0))],
        out_specs=fwd(t),
        out_shape=jax.ShapeDtypeStruct((b * t, heads * k), BF16),
        scratch_shapes=[pltpu.VMEM((t, k), F32),
                        pltpu.VMEM((t, 2 * k), BF16),
                        pltpu.VMEM((nx + nc, k, 2 * k), F32),
                        pltpu.VMEM((nx + nc, 1, 2 * k), F32),
                        pltpu.VMEM((nx, k, 2 * k), BF16),
                        pltpu.VMEM((t, 2 * k), F32)],
        compiler_params=_cparams("parallel", "parallel"),
        name="hgrn2_scan",
    )(qx, vx, kkx, kkx, lfx, lfx, gx, qc, vc, kkc, kkc, lfc, lfc, gain,
      jnp.tril(jnp.ones((SCAN_BLOCK, SCAN_BLOCK), BF16)), parity)


def _mix_kernel(n_groups, chunk, u_ref, v_ref, g_ref, b_ref, ws_ref, bias_ref, y_ref, vn_ref):
    vn_ref[...] = (_ln_rows(v_ref[...].astype(F32)) * g_ref[...] + b_ref[...]).astype(BF16)
    gc = v_ref.shape[1] // n_groups
    for cidx in range(v_ref.shape[0] // chunk):
        rows = pl.ds(cidx * chunk, chunk)
        for g in range(n_groups):
            cols = pl.ds(g * gc, gc)
            mixed = jnp.dot(ws_ref[g], vn_ref[rows, cols], preferred_element_type=F32)
            y_ref[rows, cols] = (u_ref[rows, cols].astype(F32) * (mixed + bias_ref[:, cols])).astype(BF16)


def _chunk_mix(uv, v_g, v_b, w_s, bias, rows_per_step):
    m = uv.shape[0]
    bw = uv.shape[1] // 2
    groups, chunk, _ = w_s.shape
    tm = rows_per_step
    return pl.pallas_call(
        functools.partial(_mix_kernel, groups, chunk),
        grid=(m // tm,),
        in_specs=[pl.BlockSpec((tm, bw), lambda i: (i, 0)),
                  pl.BlockSpec((tm, bw), lambda i: (i, 1)),
                  pl.BlockSpec((1, bw), lambda i: (0, 0)),
                  pl.BlockSpec((1, bw), lambda i: (0, 0)),
                  pl.BlockSpec((groups, chunk, chunk), lambda i: (0, 0, 0)),
                  pl.BlockSpec((chunk, bw), lambda i: (0, 0))],
        out_specs=pl.BlockSpec((tm, bw), lambda i: (i, 0)),
        out_shape=jax.ShapeDtypeStruct((m, bw), BF16),
        scratch_shapes=[pltpu.VMEM((tm, bw), BF16)],
        compiler_params=_cparams("parallel"),
        name="chunk_mix",
    )(uv, uv, v_g, v_b, w_s, bias)


def _merge_kernel(ya_ref, yb_ref, wa_ref, wb_ref, ga_ref, gb_ref, o_ref):
    a = jnp.dot(ya_ref[...], wa_ref[...], preferred_element_type=F32)
    bq = jnp.dot(yb_ref[...], wb_ref[...], preferred_element_type=F32)
    o_ref[...] = (ga_ref[...].astype(F32) * a + gb_ref[...].astype(F32) * bq).astype(BF16)


def _merge(ya, yb, wa, wb, gates):
    m, ka = ya.shape
    kb = yb.shape[1]
    d = wa.shape[1]
    tm, tn = min(MXU_TILE, m), min(MXU_TILE, d)
    nd = d // tn
    return pl.pallas_call(
        _merge_kernel,
        grid=(m // tm, nd),
        in_specs=[pl.BlockSpec((tm, ka), lambda i, j: (i, 0)),
                  pl.BlockSpec((tm, kb), lambda i, j: (i, 0)),
                  pl.BlockSpec((ka, tn), lambda i, j: (0, j)),
                  pl.BlockSpec((kb, tn), lambda i, j: (0, j)),
                  pl.BlockSpec((tm, tn), lambda i, j: (i, j)),
                  pl.BlockSpec((tm, tn), lambda i, j: (i, j + nd))],
        out_specs=pl.BlockSpec((tm, tn), lambda i, j: (i, j)),
        out_shape=jax.ShapeDtypeStruct((m, d), BF16),
        compiler_params=_cparams("parallel", "arbitrary"),
        name="branch_merge",
    )(ya, yb, wa, wb, gates, gates)


def _resid_kernel(alpha, x_ref, w_ref, r_ref, gate_ref, o_ref):
    y = jnp.dot(x_ref[...], w_ref[...], preferred_element_type=F32)
    o_ref[...] = alpha * r_ref[...] + gate_ref[...] * y


def _out_proj_resid(xin, w, resid, mod3, gate_idx, alpha, t):
    m, kdim = xin.shape
    d = w.shape[1]
    tm, tn = min(MXU_TILE, m, t), min(MXU_TILE, d)
    per_batch = t // tm
    nd = d // tn
    return pl.pallas_call(
        functools.partial(_resid_kernel, alpha),
        grid=(m // tm, nd),
        in_specs=[pl.BlockSpec((tm, kdim), lambda i, j: (i, 0)),
                  pl.BlockSpec((kdim, tn), lambda i, j: (0, j)),
                  pl.BlockSpec((tm, tn), lambda i, j: (i, j)),
                  pl.BlockSpec((None, 1, tn), lambda i, j: (i // per_batch, 0, gate_idx * nd + j))],
        out_specs=pl.BlockSpec((tm, tn), lambda i, j: (i, j)),
        out_shape=jax.ShapeDtypeStruct((m, d), F32),
        compiler_params=_cparams("parallel", "arbitrary"),
        name="out_proj_resid",
    )(xin, w, resid, mod3)


def _ln_adaln_kernel(p_ref, g_ref, b_ref, sh_ref, sc_ref, x_ref, h_ref):
    x = _ln_rows(p_ref[...]) * g_ref[...] + b_ref[...]
    x_ref[...] = x
    h_ref[...] = (_ln_rows(x) * (1.0 + sc_ref[...]) + sh_ref[...]).astype(BF16)


def _ln_adaln(pre, g, bvec, mod3, shift_idx, scale_idx, t, tm):
    m, d = pre.shape
    per_batch = t // tm
    row = pl.BlockSpec((tm, d), lambda i: (i, 0))
    vec = pl.BlockSpec((1, d), lambda i: (0, 0))
    return pl.pallas_call(
        _ln_adaln_kernel,
        grid=(m // tm,),
        in_specs=[row, vec, vec,
                  pl.BlockSpec((None, 1, d), lambda i: (i // per_batch, 0, shift_idx)),
                  pl.BlockSpec((None, 1, d), lambda i: (i // per_batch, 0, scale_idx))],
        out_specs=[row, row],
        out_shape=[jax.ShapeDtypeStruct((m, d), F32), jax.ShapeDtypeStruct((m, d), BF16)],
        compiler_params=_cparams("parallel"),
        name="ln_adaln",
    )(pre, g, bvec, mod3, mod3)


def _ln_kernel(p_ref, g_ref, b_ref, o_ref):
    o_ref[...] = _ln_rows(p_ref[...]) * g_ref[...] + b_ref[...]


def _ln(pre, g, bvec, tm):
    m, d = pre.shape
    row = pl.BlockSpec((tm, d), lambda i: (i, 0))
    vec = pl.BlockSpec((1, d), lambda i: (0, 0))
    return pl.pallas_call(
        _ln_kernel,
        grid=(m // tm,),
        in_specs=[row, vec, vec],
        out_specs=row,
        out_shape=jax.ShapeDtypeStruct((m, d), F32),
        compiler_params=_cparams("parallel"),
        name="ln_out",
    )(pre, g, bvec)


def _ffn1_kernel(x_ref, w_ref, o_ref):
    y = jnp.dot(x_ref[...], w_ref[...], preferred_element_type=F32)
    o_ref[...] = jnp.square(jnp.maximum(y, 0.0)).astype(BF16)


def _ffn1(h, w):
    m, d = h.shape
    n = w.shape[1]
    tm, tn = min(MXU_TILE, m), min(MXU_TILE, n)
    return pl.pallas_call(
        _ffn1_kernel,
        grid=(m // tm, n // tn),
        in_specs=[pl.BlockSpec((tm, d), lambda i, j: (i, 0)),
                  pl.BlockSpec((d, tn), lambda i, j: (0, j))],
        out_specs=pl.BlockSpec((tm, tn), lambda i, j: (i, j)),
        out_shape=jax.ShapeDtypeStruct((m, n), BF16),
        compiler_params=_cparams("parallel", "arbitrary"),
        name="ffn_up",
    )(h, w)


def _ffn2_kernel(alpha, a_ref, w_ref, r_ref, gate_ref, o_ref):
    kk = pl.program_id(2)

    @pl.when(kk == 0)
    def _():
        o_ref[...] = jnp.dot(a_ref[...], w_ref[...], preferred_element_type=F32)

    @pl.when(kk > 0)
    def _():
        o_ref[...] += jnp.dot(a_ref[...], w_ref[...], preferred_element_type=F32)

    @pl.when(kk == pl.num_programs(2) - 1)
    def _():
        o_ref[...] = alpha * r_ref[...] + gate_ref[...] * o_ref[...]


def _ffn2_resid(a, w, resid, mod3, gate_idx, alpha, t):
    m, f = a.shape
    d = w.shape[1]
    tm, tn, tk = min(MXU_TILE, m, t), min(MXU_TILE, d), min(4 * MXU_TILE, f)
    per_batch = t // tm
    nd = d // tn
    return pl.pallas_call(
        functools.partial(_ffn2_kernel, alpha),
        grid=(m // tm, nd, f // tk),
        in_specs=[pl.BlockSpec((tm, tk), lambda i, j, kk: (i, kk)),
                  pl.BlockSpec((tk, tn), lambda i, j, kk: (kk, j)),
                  pl.BlockSpec((tm, tn), lambda i, j, kk: (i, j)),
                  pl.BlockSpec((None, 1, tn), lambda i, j, kk: (i // per_batch, 0, gate_idx * nd + j))],
        out_specs=pl.BlockSpec((tm, tn), lambda i, j, kk: (i, j)),
        out_shape=jax.ShapeDtypeStruct((m, d), F32),
        compiler_params=_cparams("parallel", "parallel", "arbitrary"),
        name="ffn_down_resid",
    )(a, w, resid, mod3)


def _pos_tables(rows, cols, dim):
    quarter = dim // 4
    omega = 1.0 / (POS_BASE ** (jnp.arange(quarter, dtype=F32) / quarter))

    def emb(p):
        a = p[:, None] * omega[None, :]
        return jnp.concatenate([jnp.sin(a), jnp.cos(a)], axis=-1)

    return (emb(jnp.arange(rows, dtype=F32))[:, None, :], emb(jnp.arange(cols, dtype=F32))[None, :, :])


def kernel(x, c, ctx, c_ctx, w_ada, b_ada, w_in, lb_logits, a_norm_g, w_proj_a, v_norm_g, v_norm_b,
           w_s, b_s, w_proj_b, w_out, ln1_g, ln1_b, w_ff1, w_ff2, ln2_g, ln2_b):
    depth = w_ada.shape[0]
    assert depth == 1, "only the single-layer block is implemented"
    b, t, d = x.shape
    tc = ctx.shape[1]
    a_fdim = lb_logits.shape[-1]
    a_width = w_proj_a.shape[1]
    b_width = w_proj_b.shape[1]
    assert a_fdim == a_width and a_norm_g.shape[-1] == HEAD_DIM
    assert t % GRID_W == 0 and t % SCAN_BLOCK == 0 and tc % SCAN_BLOCK == 0
    assert w_s.shape[2] == SCAN_BLOCK or t % w_s.shape[2] == 0
    alpha = (2.0 * depth) ** 0.25
    m = b * t

    sizes = (a_fdim, a_width, a_fdim, a_fdim, a_width, b_width, b_width, d, d)
    offs = [0]
    for s in sizes:
        offs.append(offs[-1] + s)
    assert offs[-1] == w_in.shape[-1]

    w_in_b = w_in[0].astype(BF16)
    wa_b, wb_b = w_proj_a[0].astype(BF16), w_proj_b[0].astype(BF16)
    wo_b = w_out[0].astype(BF16)
    w1_b, w2_b = w_ff1[0].astype(BF16), w_ff2[0].astype(BF16)
    ws_b = w_s[0].astype(BF16)
    lb_all = jnp.cumsum(jax.nn.softmax(lb_logits.astype(F32), axis=1), axis=1)
    lb_row = jnp.concatenate([lb_all[0, 0], lb_all[1, 0]])[None, :]
    mix_bias = jnp.repeat(b_s[0].T, b_width // w_s.shape[1], axis=1)
    row_tab, col_tab = _pos_tables(t // GRID_W, GRID_W, d)

    n_rows = -(-(b + 1) // 8) * 8
    cc = jnp.concatenate([c, c_ctx[None, :], jnp.zeros((n_rows - b - 1, d), F32)], axis=0)
    mod = _modulation(cc, w_ada[0], b_ada[0][None, :])
    mod3 = mod.reshape(n_rows, 1, N_MOD * d)

    xp, h = _adaln_pos(x, row_tab, col_tab, mod3, rows_per_step=min(4, t // GRID_W))
    hc = _adaln_ctx(ctx.reshape(b * tc, d), mod3, b, tm=min(256, b * tc))

    scale = HEAD_DIM ** -0.5
    ep_q = functools.partial(_ep_silu, scale)
    ep_g = functools.partial(_ep_silu, 1.0)

    def branch_a_inputs(hh, tag):
        (q,) = _in_proj(hh, w_in_b, offs[0], sizes[0], ep_q, [BF16], name="in_proj_q" + tag)
        (v,) = _in_proj(hh, w_in_b, offs[1], sizes[1], _ep_cast, [BF16], name="in_proj_i" + tag)
        kk, lf = _in_proj(hh, w_in_b, offs[2], sizes[2] + sizes[3], _ep_decay, [BF16, F32],
                          extra=(lb_row,), name="in_proj_f" + tag)
        return q, v, kk, lf

    qx, vx, kkx, lfx = branch_a_inputs(h, "")
    qc, vc, kkc, lfc = branch_a_inputs(hc, "_ctx")
    (gx,) = _in_proj(h, w_in_b, offs[4], sizes[4], ep_g, [BF16], name="in_proj_g")
    (uv,) = _in_proj(h, w_in_b, offs[5], sizes[5] + sizes[6], _ep_cast, [BF16], name="in_proj_uv")
    (gates,) = _in_proj(h, w_in_b, offs[7], sizes[7] + sizes[8], _ep_sigmoid, [BF16], name="in_proj_gates")

    ya = _hgrn2_scan((qx, vx, kkx, lfx, gx), (qc, vc, kkc, lfc), a_norm_g[0][None, :], b, t, tc)
    yb = _chunk_mix(uv, v_norm_g[0][None, :], v_norm_b[0][None, :], ws_b, mix_bias,
                    rows_per_step=min(2 * w_s.shape[2], t))

    merged = _merge(ya, yb, wa_b, wb_b, gates)
    pre1 = _out_proj_resid(merged, wo_b, xp, mod3, 2, alpha, t)
    x1, h2 = _ln_adaln(pre1, ln1_g[0][None, :], ln1_b[0][None, :], mod3, 3, 4, t, tm=min(256, t))
    act = _ffn1(h2, w1_b)
    pre2 = _ffn2_resid(act, w2_b, x1, mod3, 5, alpha, t)
    out = _ln(pre2, ln2_g[0][None, :], ln2_b[0][None, :], tm=min(256, m))
    return out.reshape(b, t, d)
```

```python
import functools
import math
from typing import NamedTuple

import jax
import jax.numpy as jnp
from jax import lax
from jax.experimental import pallas as pl
from jax.experimental.pallas import tpu as pltpu

F32 = jnp.float32
BF16 = jnp.bfloat16

GRID_W = 64
LN_EPS = 1e-6
POS_BASE = 10000.0
N_MOD = 6
HEAD_DIM = 128
SCAN_BLOCK = 128
MXU_TILE = 1024
CAST_COLS = 4096
BF16_SUBLANES = 16
VMEM_LIMIT = 56 * 1024 * 1024


def _cparams(*sem):
    return pltpu.CompilerParams(dimension_semantics=sem, vmem_limit_bytes=VMEM_LIMIT)


def _sigmoid(x):
    return 0.5 * jnp.tanh(0.5 * x) + 0.5


def _ln_rows(x):
    mu = jnp.mean(x, axis=-1, keepdims=True)
    xc = x - mu
    return xc * lax.rsqrt(jnp.mean(xc * xc, axis=-1, keepdims=True) + LN_EPS)


def _mod_kernel(c_ref, w_ref, b_ref, o_ref):
    a = c_ref[...]
    a = a * _sigmoid(a)
    o_ref[...] = jnp.dot(a.astype(BF16), w_ref[...].astype(BF16),
                         preferred_element_type=F32) + b_ref[...]


def _modulation(cc, w_ada, b_ada):
    rows, d = cc.shape
    n = w_ada.shape[1]
    tn = min(512, n)
    return pl.pallas_call(
        _mod_kernel,
        grid=(n // tn,),
        in_specs=[pl.BlockSpec((rows, d), lambda j: (0, 0)),
                  pl.BlockSpec((d, tn), lambda j: (0, j)),
                  pl.BlockSpec((1, tn), lambda j: (0, j))],
        out_specs=pl.BlockSpec((rows, tn), lambda j: (0, j)),
        out_shape=jax.ShapeDtypeStruct((rows, n), F32),
        compiler_params=_cparams("arbitrary"),
        name="modulation",
    )(cc, w_ada, b_ada)


def _adaln_pos_kernel(x_ref, rt_ref, ct_ref, sh_ref, sc_ref, xp_ref, h_ref):
    half = rt_ref.shape[-1]
    d = 2 * half
    xa = x_ref[:, :, :half] + rt_ref[...]
    xb = x_ref[:, :, half:] + ct_ref[...]
    xp_ref[:, :, :half] = xa
    xp_ref[:, :, half:] = xb
    mu = (jnp.sum(xa, axis=-1, keepdims=True) + jnp.sum(xb, axis=-1, keepdims=True)) * (1.0 / d)
    xa = xa - mu
    xb = xb - mu
    var = (jnp.sum(xa * xa, axis=-1, keepdims=True) + jnp.sum(xb * xb, axis=-1, keepdims=True)) * (1.0 / d)
    inv = lax.rsqrt(var + LN_EPS)
    sh = sh_ref[...]
    sc = 1.0 + sc_ref[...]
    h_ref[:, :, :half] = (xa * inv * sc[:, :half] + sh[:, :half]).astype(BF16)
    h_ref[:, :, half:] = (xb * inv * sc[:, half:] + sh[:, half:]).astype(BF16)


def _adaln_pos(x, row_tab, col_tab, mod3, rows_per_step):
    b, t, d = x.shape
    gr = t // GRID_W
    r = rows_per_step
    x4 = x.reshape(b, gr, GRID_W, d)
    blk = pl.BlockSpec((None, r, GRID_W, d), lambda i, j: (i, j, 0, 0))
    xp, h = pl.pallas_call(
        _adaln_pos_kernel,
        grid=(b, gr // r),
        in_specs=[blk,
                  pl.BlockSpec((r, 1, d // 2), lambda i, j: (j, 0, 0)),
                  pl.BlockSpec((1, GRID_W, d // 2), lambda i, j: (0, 0, 0)),
                  pl.BlockSpec((None, 1, d), lambda i, j: (i, 0, 0)),
                  pl.BlockSpec((None, 1, d), lambda i, j: (i, 0, 1))],
        out_specs=[blk, blk],
        out_shape=[jax.ShapeDtypeStruct(x4.shape, F32), jax.ShapeDtypeStruct(x4.shape, BF16)],
        compiler_params=_cparams("parallel", "parallel"),
        name="adaln_pos",
    )(x4, row_tab, col_tab, mod3, mod3)
    return xp.reshape(b * t, d), h.reshape(b * t, d)


def _adaln_rows_kernel(x_ref, sh_ref, sc_ref, h_ref):
    y = _ln_rows(x_ref[...])
    h_ref[...] = (y * (1.0 + sc_ref[...]) + sh_ref[...]).astype(BF16)


def _adaln_ctx(xc, mod3, ctx_row, tm):
    m, d = xc.shape
    return pl.pallas_call(
        _adaln_rows_kernel,
        grid=(m // tm,),
        in_specs=[pl.BlockSpec((tm, d), lambda i: (i, 0)),
                  pl.BlockSpec((None, 1, d), lambda i: (ctx_row, 0, 0)),
                  pl.BlockSpec((None, 1, d), lambda i: (ctx_row, 0, 1))],
        out_specs=pl.BlockSpec((tm, d), lambda i: (i, 0)),
        out_shape=jax.ShapeDtypeStruct((m, d), BF16),
        compiler_params=_cparams("parallel"),
        name="adaln_ctx",
    )(xc, mod3, mod3)


def _ep_silu(scale, acc, o_ref):
    o_ref[...] = (acc * _sigmoid(acc) * scale).astype(o_ref.dtype)


def _ep_cast(acc, o_ref):
    o_ref[...] = acc.astype(o_ref.dtype)


def _ep_sigmoid(acc, o_ref):
    o_ref[...] = _sigmoid(acc).astype(o_ref.dtype)


def _ep_relu2(acc, o_ref):
    o_ref[...] = jnp.square(jnp.maximum(acc, 0.0)).astype(o_ref.dtype)


def _ep_decay(acc, lb_ref, k_ref, lf_ref):
    lb = lb_ref[...]
    th = 0.5 * jnp.tanh(0.5 * acc)
    one_lb = 1.0 - lb
    k_ref[...] = (one_lb * (0.5 - th)).astype(k_ref.dtype)
    lf_ref[...] = jnp.log2(lb + one_lb * (0.5 + th))


class _Cast(NamedTuple):
    src: jax.Array
    col0: int
    width: int


def _cast_blocks(cast, steps):
    cw = math.gcd(cast.col0, cast.width, CAST_COLS)
    ncb = cast.width // cw
    return (cast.src.shape[0] * ncb) // steps, cw, ncb


def _cast_rides(cast, steps):
    rb, _, ncb = _cast_blocks(cast, steps)
    rows = cast.src.shape[0]
    return rb * steps == rows * ncb and rb % BF16_SUBLANES == 0 and rows % rb == 0


def _split_casts(casts, steps):
    return [c for c in casts if _cast_rides(c, steps)]


def _cast_results(casts, steps, ridden):
    ridden = list(ridden)
    return [ridden.pop(0) if _cast_rides(c, steps) else c.src[:, c.col0:c.col0 + c.width].astype(BF16)
            for c in casts]


def _cast_specs(casts, n_i, n_j):
    steps = n_i * n_j
    in_specs, out_specs, out_shapes = [], [], []
    for cast in casts:
        src, col0, width = cast
        rows = src.shape[0]
        rb, cw, ncb = _cast_blocks(cast, steps)
        coff = col0 // cw
        in_specs.append(pl.BlockSpec(
            (rb, cw), lambda i, j, ncb=ncb, coff=coff: ((i * n_j + j) // ncb, coff + (i * n_j + j) % ncb)))
        out_specs.append(pl.BlockSpec(
            (rb, cw), lambda i, j, ncb=ncb: ((i * n_j + j) // ncb, (i * n_j + j) % ncb)))
        out_shapes.append(jax.ShapeDtypeStruct((rows, width), BF16))
    return in_specs, out_specs, out_shapes


def _do_casts(src_refs, dst_refs):
    for s_ref, d_ref in zip(src_refs, dst_refs, strict=True):
        d_ref[...] = s_ref[...].astype(BF16)


def _proj_kernel(epilogue, n_extra, n_cast, x_ref, w_ref, *refs):
    extras, refs = refs[:n_extra], refs[n_extra:]
    cast_src, refs = refs[:n_cast], refs[n_cast:]
    outs, cast_dst = refs[:len(refs) - n_cast], refs[len(refs) - n_cast:]
    acc = jnp.dot(x_ref[...], w_ref[...], preferred_element_type=F32)
    epilogue(acc, *extras, *outs)
    _do_casts(cast_src, cast_dst)


def _in_proj(h, w, epilogue, out_dtypes, extra=(), casts=(), name="in_proj"):
    m, d = h.shape
    width = w.shape[1]
    tm = min(MXU_TILE, m)
    tn = min(MXU_TILE, width)
    n_i, n_j = m // tm, width // tn
    out_blk = pl.BlockSpec((tm, tn), lambda i, j: (i, j))
    riding = _split_casts(casts, n_i * n_j)
    c_in, c_out, c_shapes = _cast_specs(riding, n_i, n_j)
    res = pl.pallas_call(
        functools.partial(_proj_kernel, epilogue, len(extra), len(riding)),
        grid=(n_i, n_j),
        in_specs=[pl.BlockSpec((tm, d), lambda i, j: (i, 0)),
                  pl.BlockSpec((d, tn), lambda i, j: (0, j))]
                 + [pl.BlockSpec((1, tn), lambda i, j: (0, j)) for _ in extra] + c_in,
        out_specs=[out_blk for _ in out_dtypes] + c_out,
        out_shape=[jax.ShapeDtypeStruct((m, width), dt) for dt in out_dtypes] + c_shapes,
        compiler_params=_cparams("parallel", "arbitrary"),
        name=name,
    )(h, w, *extra, *[c.src for c in riding])
    n_out = len(out_dtypes)
    return list(res[:n_out]) + _cast_results(casts, n_i * n_j, res[n_out:])


def _mid_rows(b, m):
    c, w = b.shape
    assert m in (1, 2, 4)
    if m == 1:
        odd = (lax.broadcasted_iota(jnp.int32, (c, w), 0) & 1) == 1
        return jnp.where(odd, pltpu.roll(b, 1, 0), b)
    b3 = b.reshape(c // 8, 8, w)
    if m == 4:
        return jnp.broadcast_to(b3[:, 3:4, :], b3.shape).reshape(c, w)
    low = lax.broadcasted_iota(jnp.int32, b3.shape, 1) < 4
    return jnp.where(low, jnp.broadcast_to(b3[:, 1:2, :], b3.shape),
                     jnp.broadcast_to(b3[:, 5:6, :], b3.shape)).reshape(c, w)


def _block_sums(tri, lf, lb):
    k = lf.shape[1]
    parts = []
    for a in (lf, lb):
        hi = a.astype(BF16)
        parts += [hi, (a - hi.astype(F32)).astype(BF16)]
    cs = jnp.dot(tri, jnp.concatenate(parts, axis=1), preferred_element_type=F32)
    return jnp.concatenate([cs[:, :k] + cs[:, k:2 * k], cs[:, 2 * k:3 * k] + cs[:, 3 * k:]], axis=1)


def _state_terms(v, kf, kb, bcat, lb):
    c, k = lb.shape
    tot = bcat[c - 1:c]
    ks = jnp.concatenate([kf * jnp.exp2(tot[:, :k] - bcat[:, :k]),
                          kb * jnp.exp2(bcat[:, k:] - lb)], axis=1)
    return pl.dot(v, ks.astype(BF16), trans_a=True), jnp.exp2(tot)


def _level_products(q, kf, kb, bcat, lb, par_ref):
    c, k = lb.shape
    xf, xb = bcat[:, :k], bcat[:, k:] - lb
    prods = []
    m, level = 1, 0
    while m < c:
        if m % 8 == 0:
            split = lambda a: a.reshape(c // (2 * m), 2, m, a.shape[-1])
            join = lambda even, odd: jnp.concatenate([even, odd], axis=1).reshape(c, k)
            xf4, xb4, kf4, kb4 = split(xf), split(xb), split(kf), split(kb)
            mid = split(bcat)[:, 0:1, m - 1:m, :]
            mf, mb = mid[..., :k], mid[..., k:]
            eq = join(mb - xb4[:, 0:1], xf4[:, 1:2] - mf)
            ek = join(mf - xf4[:, 0:1], xb4[:, 1:2] - mb)
            ksel = join(kf4[:, 0:1], kb4[:, 1:2])
        else:
            mid = _mid_rows(bcat, m)
            df = xf - mid[:, :k]
            db = mid[:, k:] - xb
            eq = jnp.minimum(df, db)
            ek = -jnp.maximum(df, db)
            ksel = jnp.where(par_ref[level] > 0, kb, kf)
        prods.append(pl.dot(q * jnp.exp2(eq), ksel * jnp.exp2(ek), trans_b=True))
        m *= 2
        level += 1
    prods.append(pl.dot(q, kf + kb, trans_b=True))
    return prods


def _assemble(prods):
    c = prods[0].shape[0]
    x = (lax.broadcasted_iota(jnp.int32, (c, c), 0) ^ lax.broadcasted_iota(jnp.int32, (c, c), 1))
    att = prods[-2]
    m = c // 2
    for a in reversed(prods[:-2]):
        att = jnp.where(x < m, a, att)
        m //= 2
    return jnp.where(x < 1, prods[-1], att).astype(BF16)


def _entry_factors(q, bcat, lb):
    c, k = lb.shape
    xb = bcat[:, k:] - lb
    return jnp.concatenate([q * jnp.exp2(bcat[:, :k]),
                            q * jnp.exp2(bcat[c - 1:c, k:] - xb)], axis=1).astype(BF16)


def _scan_kernel(nx, nc,
                 qx, vx, kfx, kbx, lfx, lbx, gx, qc, vc, kfc, kbc, lfc, lbc, gain_ref, tri_ref, par_ref,
                 y_ref, oi_ref, qi_ref, u_ref, d_ref, s_ref, b_ref):
    c = SCAN_BLOCK
    k = HEAD_DIM
    f32 = lambda ref, rows: ref[rows, :].astype(F32)

    for j in range(nc):
        rows = pl.ds(j * c, c)
        bcat = _block_sums(tri_ref[...], lfc[rows, :], lbc[rows, :])
        u_t, dec = _state_terms(vc[rows, :], f32(kfc, rows), f32(kbc, rows), bcat, lbc[rows, :])
        u_ref[j] = u_t
        d_ref[j] = dec

    def sums_body(n, carry):
        rows = pl.ds(pl.multiple_of(n * c, c), c)
        b_ref[rows, :] = _block_sums(tri_ref[...], lfx[rows, :], lbx[rows, :])
        return carry

    lax.fori_loop(0, nx, sums_body, 0, unroll=math.gcd(nx, 4))

    group = math.gcd(nx, 4)

    def group_body(p, carry):
        rows = [pl.ds(pl.multiple_of((p * group + i) * c, c), c) for i in range(group)]
        qkk = [(f32(qx, rw), f32(kfx, rw), f32(kbx, rw)) for rw in rows]
        prods = [_level_products(q, kf, kb, b_ref[rw, :], lbx[rw, :], par_ref)
                 for (q, kf, kb), rw in zip(qkk, rows)]
        for i, ((_, kf, kb), rw) in enumerate(zip(qkk, rows)):
            u_t, dec = _state_terms(vx[rw, :], kf, kb, b_ref[rw, :], lbx[rw, :])
            u_ref[nc + p * group + i] = u_t
            d_ref[nc + p * group + i] = dec
        outs = [jnp.dot(_assemble(pr), vx[rw, :], preferred_element_type=F32) for pr, rw in zip(prods, rows)]
        for (q, _, _), rw in zip(qkk, rows):
            qi_ref[rw, :] = _entry_factors(q, b_ref[rw, :], lbx[rw, :])
        for o, rw in zip(outs, rows):
            oi_ref[rw, :] = o
        return carry

    lax.fori_loop(0, nx // group, group_body, 0)

    sf = jnp.zeros((k, k), F32)
    sb = jnp.zeros((k, k), F32)
    for j in range(nc):
        sf = d_ref[j][:, :k] * sf + u_ref[j][:, :k]
        jb = nc - 1 - j
        sb = d_ref[jb][:, k:] * sb + u_ref[jb][:, k:]

    def sweep_body(i, carry):
        sf, sb = carry
        s_ref[i, :, :k] = sf.astype(BF16)
        sf = d_ref[nc + i][:, :k] * sf + u_ref[nc + i][:, :k]
        j = nx - 1 - i
        s_ref[j, :, k:] = sb.astype(BF16)
        sb = d_ref[nc + j][:, k:] * sb + u_ref[nc + j][:, k:]
        return sf, sb

    lax.fori_loop(0, nx, sweep_body, (sf, sb))

    gain = gain_ref[...]

    def out_body(n, carry):
        rows = pl.ds(pl.multiple_of(n * c, c), c)
        o = oi_ref[rows, :] + pl.dot(qi_ref[rows, :], s_ref[n], trans_b=True)
        o = o * lax.rsqrt(jnp.mean(o * o, axis=-1, keepdims=True) + LN_EPS) * gain
        y_ref[rows, :] = (o * gx[rows, :].astype(F32)).astype(BF16)
        return carry

    lax.fori_loop(0, nx, out_body, 0, unroll=math.gcd(nx, 8))


def _hgrn2_scan(px, pc, gain, b, t, tc):
    qx, vx, kkx, lfx, gx = px
    qc, vc, kkc, lfc = pc
    k = HEAD_DIM
    heads = qx.shape[1] // k
    nx, nc = t // SCAN_BLOCK, tc // SCAN_BLOCK
    n_levels = SCAN_BLOCK.bit_length() - 1
    bits = (jnp.arange(SCAN_BLOCK)[None, :] >> jnp.arange(n_levels)[:, None]) & 1
    parity = jnp.broadcast_to(bits[:, :, None], (n_levels, SCAN_BLOCK, k)).astype(F32)
    fwd = lambda rows: pl.BlockSpec((rows, k), lambda i, h: (i, h))
    bwd = lambda rows: pl.BlockSpec((rows, k), lambda i, h: (i, h + heads))
    return pl.pallas_call(
        functools.partial(_scan_kernel, nx, nc),
        grid=(b, heads),
        in_specs=[fwd(t), fwd(t), fwd(t), bwd(t), fwd(t), bwd(t), fwd(t),
                  fwd(tc), fwd(tc), fwd(tc), bwd(tc), fwd(tc), bwd(tc),
                  pl.BlockSpec((1, k), lambda i, h: (0, 0)),
                  pl.BlockSpec((SCAN_BLOCK, SCAN_BLOCK), lambda i, h: (0, 0)),
                  pl.BlockSpec(parity.shape, lambda i, h: (0, 0, 0))],
        out_specs=fwd(t),
        out_shape=jax.ShapeDtypeStruct((b * t, heads * k), BF16),
        scratch_shapes=[pltpu.VMEM((t, k), F32),
                        pltpu.VMEM((t, 2 * k), BF16),
                        pltpu.VMEM((nx + nc, k, 2 * k), F32),
                        pltpu.VMEM((nx + nc, 1, 2 * k), F32),
                        pltpu.VMEM((nx, k, 2 * k), BF16),
                        pltpu.VMEM((t, 2 * k), F32)],
        compiler_params=_cparams("parallel", "parallel"),
        name="hgrn2_scan",
    )(qx, vx, kkx, kkx, lfx, lfx, gx, qc, vc, kkc, kkc, lfc, lfc, gain,
      jnp.tril(jnp.ones((SCAN_BLOCK, SCAN_BLOCK), BF16)), parity)


def _mix_kernel(n_groups, chunk, u_ref, v_ref, g_ref, b_ref, ws_ref, bias_ref, y_ref, vn_ref):
    vn_ref[...] = (_ln_rows(v_ref[...].astype(F32)) * g_ref[...] + b_ref[...]).astype(BF16)
    gc = v_ref.shape[1] // n_groups
    for cidx in range(v_ref.shape[0] // chunk):
        rows = pl.ds(cidx * chunk, chunk)
        for g in range(n_groups):
            cols = pl.ds(g * gc, gc)
            mixed = jnp.dot(ws_ref[g], vn_ref[rows, cols], preferred_element_type=F32)
            y_ref[rows, cols] = (u_ref[rows, cols].astype(F32) * (mixed + bias_ref[:, cols])).astype(BF16)


def _chunk_mix(uv, v_g, v_b, w_s, bias, rows_per_step):
    m = uv.shape[0]
    bw = uv.shape[1] // 2
    groups, chunk, _ = w_s.shape
    tm = rows_per_step
    return pl.pallas_call(
        functools.partial(_mix_kernel, groups, chunk),
        grid=(m // tm,),
        in_specs=[pl.BlockSpec((tm, bw), lambda i: (i, 0)),
                  pl.BlockSpec((tm, bw), lambda i: (i, 1)),
                  pl.BlockSpec((1, bw), lambda i: (0, 0)),
                  pl.BlockSpec((1, bw), lambda i: (0, 0)),
                  pl.BlockSpec((groups, chunk, chunk), lambda i: (0, 0, 0)),
                  pl.BlockSpec((chunk, bw), lambda i: (0, 0))],
        out_specs=pl.BlockSpec((tm, bw), lambda i: (i, 0)),
        out_shape=jax.ShapeDtypeStruct((m, bw), BF16),
        scratch_shapes=[pltpu.VMEM((tm, bw), BF16)],
        compiler_params=_cparams("parallel"),
        name="chunk_mix",
    )(uv, uv, v_g, v_b, w_s, bias)


def _merge_kernel(n_cast, ya_ref, yb_ref, wa_ref, wb_ref, ga_ref, gb_ref, *refs):
    cast_src, o_ref, cast_dst = refs[:n_cast], refs[n_cast], refs[n_cast + 1:]
    a = jnp.dot(ya_ref[...], wa_ref[...], preferred_element_type=F32)
    bq = jnp.dot(yb_ref[...], wb_ref[...], preferred_element_type=F32)
    o_ref[...] = (ga_ref[...].astype(F32) * a + gb_ref[...].astype(F32) * bq).astype(BF16)
    _do_casts(cast_src, cast_dst)


def _merge(ya, yb, wa, wb, gates, casts=()):
    m, ka = ya.shape
    kb = yb.shape[1]
    d = wa.shape[1]
    tm, tn = min(MXU_TILE, m), min(MXU_TILE, d)
    nd = d // tn
    steps = (m // tm) * nd
    riding = _split_casts(casts, steps)
    c_in, c_out, c_shapes = _cast_specs(riding, m // tm, nd)
    res = pl.pallas_call(
        functools.partial(_merge_kernel, len(riding)),
        grid=(m // tm, nd),
        in_specs=[pl.BlockSpec((tm, ka), lambda i, j: (i, 0)),
                  pl.BlockSpec((tm, kb), lambda i, j: (i, 0)),
                  pl.BlockSpec((ka, tn), lambda i, j: (0, j)),
                  pl.BlockSpec((kb, tn), lambda i, j: (0, j)),
                  pl.BlockSpec((tm, tn), lambda i, j: (i, j)),
                  pl.BlockSpec((tm, tn), lambda i, j: (i, j + nd))] + c_in,
        out_specs=[pl.BlockSpec((tm, tn), lambda i, j: (i, j))] + c_out,
        out_shape=[jax.ShapeDtypeStruct((m, d), BF16)] + c_shapes,
        compiler_params=_cparams("parallel", "arbitrary"),
        name="branch_merge",
    )(ya, yb, wa, wb, gates, gates, *[c.src for c in riding])
    return [res[0]] + _cast_results(casts, steps, res[1:])


def _resid_kernel(alpha, x_ref, w_ref, r_ref, gate_ref, o_ref):
    y = jnp.dot(x_ref[...], w_ref[...], preferred_element_type=F32)
    o_ref[...] = alpha * r_ref[...] + gate_ref[...] * y


def _out_proj_resid(xin, w, resid, mod3, gate_idx, alpha, t):
    m, kdim = xin.shape
    d = w.shape[1]
    tm, tn = min(MXU_TILE, m, t), min(MXU_TILE, d)
    per_batch = t // tm
    nd = d // tn
    return pl.pallas_call(
        functools.partial(_resid_kernel, alpha),
        grid=(m // tm, nd),
        in_specs=[pl.BlockSpec((tm, kdim), lambda i, j: (i, 0)),
                  pl.BlockSpec((kdim, tn), lambda i, j: (0, j)),
                  pl.BlockSpec((tm, tn), lambda i, j: (i, j)),
                  pl.BlockSpec((None, 1, tn), lambda i, j: (i // per_batch, 0, gate_idx * nd + j))],
        out_specs=pl.BlockSpec((tm, tn), lambda i, j: (i, j)),
        out_shape=jax.ShapeDtypeStruct((m, d), F32),
        compiler_params=_cparams("parallel", "arbitrary"),
        name="out_proj_resid",
    )(xin, w, resid, mod3)


def _ln_adaln_kernel(p_ref, g_ref, b_ref, sh_ref, sc_ref, x_ref, h_ref):
    x = _ln_rows(p_ref[...]) * g_ref[...] + b_ref[...]
    x_ref[...] = x
    h_ref[...] = (_ln_rows(x) * (1.0 + sc_ref[...]) + sh_ref[...]).astype(BF16)


def _ln_adaln(pre, g, bvec, mod3, shift_idx, scale_idx, t, tm):
    m, d = pre.shape
    per_batch = t // tm
    row = pl.BlockSpec((tm, d), lambda i: (i, 0))
    vec = pl.BlockSpec((1, d), lambda i: (0, 0))
    return pl.pallas_call(
        _ln_adaln_kernel,
        grid=(m // tm,),
        in_specs=[row, vec, vec,
                  pl.BlockSpec((None, 1, d), lambda i: (i // per_batch, 0, shift_idx)),
                  pl.BlockSpec((None, 1, d), lambda i: (i // per_batch, 0, scale_idx))],
        out_specs=[row, row],
        out_shape=[jax.ShapeDtypeStruct((m, d), F32), jax.ShapeDtypeStruct((m, d), BF16)],
        compiler_params=_cparams("parallel"),
        name="ln_adaln",
    )(pre, g, bvec, mod3, mod3)


def _ln_kernel(p_ref, g_ref, b_ref, o_ref):
    o_ref[...] = _ln_rows(p_ref[...]) * g_ref[...] + b_ref[...]


def _ln(pre, g, bvec, tm):
    m, d = pre.shape
    row = pl.BlockSpec((tm, d), lambda i: (i, 0))
    vec = pl.BlockSpec((1, d), lambda i: (0, 0))
    return pl.pallas_call(
        _ln_kernel,
        grid=(m // tm,),
        in_specs=[row, vec, vec],
        out_specs=row,
        out_shape=jax.ShapeDtypeStruct((m, d), F32),
        compiler_params=_cparams("parallel"),
        name="ln_out",
    )(pre, g, bvec)


def _ffn2_kernel(alpha, a_ref, w_ref, r_ref, gate_ref, o_ref):
    kk = pl.program_id(2)

    @pl.when(kk == 0)
    def _():
        o_ref[...] = jnp.dot(a_ref[...], w_ref[...], preferred_element_type=F32)

    @pl.when(kk > 0)
    def _():
        o_ref[...] += jnp.dot(a_ref[...], w_ref[...], preferred_element_type=F32)

    @pl.when(kk == pl.num_programs(2) - 1)
    def _():
        o_ref[...] = alpha * r_ref[...] + gate_ref[...] * o_ref[...]


def _ffn2_resid(a, w, resid, mod3, gate_idx, alpha, t):
    m, f = a.shape
    d = w.shape[1]
    tm, tn, tk = min(MXU_TILE, m, t), min(MXU_TILE, d), min(4 * MXU_TILE, f)
    per_batch = t // tm
    nd = d // tn
    return pl.pallas_call(
        functools.partial(_ffn2_kernel, alpha),
        grid=(m // tm, nd, f // tk),
        in_specs=[pl.BlockSpec((tm, tk), lambda i, j, kk: (i, kk)),
                  pl.BlockSpec((tk, tn), lambda i, j, kk: (kk, j)),
                  pl.BlockSpec((tm, tn), lambda i, j, kk: (i, j)),
                  pl.BlockSpec((None, 1, tn), lambda i, j, kk: (i // per_batch, 0, gate_idx * nd + j))],
        out_specs=pl.BlockSpec((tm, tn), lambda i, j, kk: (i, j)),
        out_shape=jax.ShapeDtypeStruct((m, d), F32),
        compiler_params=_cparams("parallel", "parallel", "arbitrary"),
        name="ffn_down_resid",
    )(a, w, resid, mod3)


def _pos_tables(rows, cols, dim):
    quarter = dim // 4
    omega = 1.0 / (POS_BASE ** (jnp.arange(quarter, dtype=F32) / quarter))

    def emb(p):
        a = p[:, None] * omega[None, :]
        return jnp.concatenate([jnp.sin(a), jnp.cos(a)], axis=-1)

    return (emb(jnp.arange(rows, dtype=F32))[:, None, :], emb(jnp.arange(cols, dtype=F32))[None, :, :])


def kernel(x, c, ctx, c_ctx, w_ada, b_ada, w_in, lb_logits, a_norm_g, w_proj_a, v_norm_g, v_norm_b,
           w_s, b_s, w_proj_b, w_out, ln1_g, ln1_b, w_ff1, w_ff2, ln2_g, ln2_b):
    depth = w_ada.shape[0]
    assert depth == 1, "only the single-layer block is implemented"
    b, t, d = x.shape
    tc = ctx.shape[1]
    a_fdim = lb_logits.shape[-1]
    a_width = w_proj_a.shape[1]
    b_width = w_proj_b.shape[1]
    assert a_fdim == a_width and a_norm_g.shape[-1] == HEAD_DIM
    assert t % GRID_W == 0 and t % SCAN_BLOCK == 0 and tc % SCAN_BLOCK == 0
    assert w_s.shape[2] == SCAN_BLOCK or t % w_s.shape[2] == 0
    alpha = (2.0 * depth) ** 0.25
    m = b * t

    sizes = (a_fdim, a_width, a_fdim, a_fdim, a_width, b_width, b_width, d, d)
    offs = [0]
    for s in sizes:
        offs.append(offs[-1] + s)
    assert offs[-1] == w_in.shape[-1]

    w_in0 = w_in[0]
    seg = lambda first, last: _Cast(w_in0, offs[first], offs[last + 1] - offs[first])
    wq_b = w_in0[:, :offs[1]].astype(BF16)
    ws_b = w_s[0].astype(BF16)
    lb_all = jnp.cumsum(jax.nn.softmax(lb_logits.astype(F32), axis=1), axis=1)
    lb_row = jnp.concatenate([lb_all[0, 0], lb_all[1, 0]])[None, :]
    mix_bias = jnp.repeat(b_s[0].T, b_width // w_s.shape[1], axis=1)
    row_tab, col_tab = _pos_tables(t // GRID_W, GRID_W, d)

    n_rows = -(-(b + 1) // 8) * 8
    cc = jnp.concatenate([c, c_ctx[None, :], jnp.zeros((n_rows - b - 1, d), F32)], axis=0)
    mod = _modulation(cc, w_ada[0], b_ada[0][None, :])
    mod3 = mod.reshape(n_rows, 1, N_MOD * d)

    xp, h = _adaln_pos(x, row_tab, col_tab, mod3, rows_per_step=min(4, t // GRID_W))
    hc = _adaln_ctx(ctx.reshape(b * tc, d), mod3, b, tm=min(256, b * tc))

    scale = HEAD_DIM ** -0.5
    ep_q = functools.partial(_ep_silu, scale)
    ep_g = functools.partial(_ep_silu, 1.0)

    qx, wi_b = _in_proj(h, wq_b, ep_q, [BF16], casts=(seg(1, 1),), name="in_proj_q")
    vx, wf_b = _in_proj(h, wi_b, _ep_cast, [BF16], casts=(seg(2, 3),), name="in_proj_i")
    kkx, lfx, wg_b = _in_proj(h, wf_b, _ep_decay, [BF16, F32], extra=(lb_row,), casts=(seg(4, 4),),
                              name="in_proj_f")
    gx, wuv_b = _in_proj(h, wg_b, ep_g, [BF16], casts=(seg(5, 6),), name="in_proj_g")
    uv, wgate_b = _in_proj(h, wuv_b, _ep_cast, [BF16], casts=(seg(7, 8),), name="in_proj_uv")
    gates, wa_b, wb_b, w1_b = _in_proj(
        h, wgate_b, _ep_sigmoid, [BF16],
        casts=(_Cast(w_proj_a[0], 0, d), _Cast(w_proj_b[0], 0, d), _Cast(w_ff1[0], 0, w_ff1.shape[-1])),
        name="in_proj_gates")

    (qc,) = _in_proj(hc, wq_b, ep_q, [BF16], name="in_proj_q_ctx")
    (vc,) = _in_proj(hc, wi_b, _ep_cast, [BF16], name="in_proj_i_ctx")
    kkc, lfc = _in_proj(hc, wf_b, _ep_decay, [BF16, F32], extra=(lb_row,), name="in_proj_f_ctx")

    ya = _hgrn2_scan((qx, vx, kkx, lfx, gx), (qc, vc, kkc, lfc), a_norm_g[0][None, :], b, t, tc)
    yb = _chunk_mix(uv, v_norm_g[0][None, :], v_norm_b[0][None, :], ws_b, mix_bias,
                    rows_per_step=min(2 * w_s.shape[2], t))

    merged, wo_b = _merge(ya, yb, wa_b, wb_b, gates, casts=(_Cast(w_out[0], 0, d),))
    pre1 = _out_proj_resid(merged, wo_b, xp, mod3, 2, alpha, t)
    x1, h2 = _ln_adaln(pre1, ln1_g[0][None, :], ln1_b[0][None, :], mod3, 3, 4, t, tm=min(256, t))
    act, w2_b = _in_proj(h2, w1_b, _ep_relu2, [BF16], casts=(_Cast(w_ff2[0], 0, d),), name="ffn_up")
    pre2 = _ffn2_resid(act, w2_b, x1, mod3, 5, alpha, t)
    out = _ln(pre2, ln2_g[0][None, :], ln2_b[0][None, :], tm=min(256, m))
    return out.reshape(b, t, d)
```

```python
import functools
import math
from typing import NamedTuple

import jax
import jax.numpy as jnp
from jax import lax
from jax.experimental import pallas as pl
from jax.experimental.pallas import tpu as pltpu

F32 = jnp.float32
BF16 = jnp.bfloat16

GRID_W = 64
LN_EPS = 1e-6
POS_BASE = 10000.0
N_MOD = 6
HEAD_DIM = 128
SCAN_BLOCK = 128
MXU_TILE = 1024
CAST_COLS = 4096
BF16_SUBLANES = 16
VMEM_LIMIT = 56 * 1024 * 1024


def _cparams(*sem):
    return pltpu.CompilerParams(dimension_semantics=sem, vmem_limit_bytes=VMEM_LIMIT)


def _sigmoid(x):
    return 0.5 * jnp.tanh(0.5 * x) + 0.5


def _ln_rows(x):
    mu = jnp.mean(x, axis=-1, keepdims=True)
    xc = x - mu
    return xc * lax.rsqrt(jnp.mean(xc * xc, axis=-1, keepdims=True) + LN_EPS)


def _mod_kernel(c_ref, w_ref, b_ref, o_ref):
    a = c_ref[...]
    a = a * _sigmoid(a)
    o_ref[...] = jnp.dot(a.astype(BF16), w_ref[...].astype(BF16),
                         preferred_element_type=F32) + b_ref[...]


def _modulation(cc, w_ada, b_ada):
    rows, d = cc.shape
    n = w_ada.shape[1]
    tn = min(512, n)
    return pl.pallas_call(
        _mod_kernel,
        grid=(n // tn,),
        in_specs=[pl.BlockSpec((rows, d), lambda j: (0, 0)),
                  pl.BlockSpec((d, tn), lambda j: (0, j)),
                  pl.BlockSpec((1, tn), lambda j: (0, j))],
        out_specs=pl.BlockSpec((rows, tn), lambda j: (0, j)),
        out_shape=jax.ShapeDtypeStruct((rows, n), F32),
        compiler_params=_cparams("arbitrary"),
        name="modulation",
    )(cc, w_ada, b_ada)


def _adaln_pos_kernel(x_ref, rt_ref, ct_ref, sh_ref, sc_ref, xp_ref, h_ref):
    half = rt_ref.shape[-1]
    d = 2 * half
    xa = x_ref[:, :, :half] + rt_ref[...]
    xb = x_ref[:, :, half:] + ct_ref[...]
    xp_ref[:, :, :half] = xa
    xp_ref[:, :, half:] = xb
    mu = (jnp.sum(xa, axis=-1, keepdims=True) + jnp.sum(xb, axis=-1, keepdims=True)) * (1.0 / d)
    xa = xa - mu
    xb = xb - mu
    var = (jnp.sum(xa * xa, axis=-1, keepdims=True) + jnp.sum(xb * xb, axis=-1, keepdims=True)) * (1.0 / d)
    inv = lax.rsqrt(var + LN_EPS)
    sh = sh_ref[...]
    sc = 1.0 + sc_ref[...]
    h_ref[:, :, :half] = (xa * inv * sc[:, :half] + sh[:, :half]).astype(BF16)
    h_ref[:, :, half:] = (xb * inv * sc[:, half:] + sh[:, half:]).astype(BF16)


def _adaln_pos(x, row_tab, col_tab, mod3, rows_per_step):
    b, t, d = x.shape
    gr = t // GRID_W
    r = rows_per_step
    x4 = x.reshape(b, gr, GRID_W, d)
    blk = pl.BlockSpec((None, r, GRID_W, d), lambda i, j: (i, j, 0, 0))
    xp, h = pl.pallas_call(
        _adaln_pos_kernel,
        grid=(b, gr // r),
        in_specs=[blk,
                  pl.BlockSpec((r, 1, d // 2), lambda i, j: (j, 0, 0)),
                  pl.BlockSpec((1, GRID_W, d // 2), lambda i, j: (0, 0, 0)),
                  pl.BlockSpec((None, 1, d), lambda i, j: (i, 0, 0)),
                  pl.BlockSpec((None, 1, d), lambda i, j: (i, 0, 1))],
        out_specs=[blk, blk],
        out_shape=[jax.ShapeDtypeStruct(x4.shape, F32), jax.ShapeDtypeStruct(x4.shape, BF16)],
        compiler_params=_cparams("parallel", "parallel"),
        name="adaln_pos",
    )(x4, row_tab, col_tab, mod3, mod3)
    return xp.reshape(b * t, d), h.reshape(b * t, d)


def _adaln_rows_kernel(x_ref, sh_ref, sc_ref, h_ref):
    y = _ln_rows(x_ref[...])
    h_ref[...] = (y * (1.0 + sc_ref[...]) + sh_ref[...]).astype(BF16)


def _adaln_ctx(xc, mod3, ctx_row, tm):
    m, d = xc.shape
    return pl.pallas_call(
        _adaln_rows_kernel,
        grid=(m // tm,),
        in_specs=[pl.BlockSpec((tm, d), lambda i: (i, 0)),
                  pl.BlockSpec((None, 1, d), lambda i: (ctx_row, 0, 0)),
                  pl.BlockSpec((None, 1, d), lambda i: (ctx_row, 0, 1))],
        out_specs=pl.BlockSpec((tm, d), lambda i: (i, 0)),
        out_shape=jax.ShapeDtypeStruct((m, d), BF16),
        compiler_params=_cparams("parallel"),
        name="adaln_ctx",
    )(xc, mod3, mod3)


def _ep_silu(scale, acc, o_ref):
    o_ref[...] = (acc * _sigmoid(acc) * scale).astype(o_ref.dtype)


def _ep_cast(acc, o_ref):
    o_ref[...] = acc.astype(o_ref.dtype)


def _ep_sigmoid(acc, o_ref):
    o_ref[...] = _sigmoid(acc).astype(o_ref.dtype)


def _ep_relu2(acc, o_ref):
    o_ref[...] = jnp.square(jnp.maximum(acc, 0.0)).astype(o_ref.dtype)


def _ep_decay(acc, lb_ref, k_ref, lf_ref):
    lb = lb_ref[...]
    th = 0.5 * jnp.tanh(0.5 * acc)
    one_lb = 1.0 - lb
    k_ref[...] = (one_lb * (0.5 - th)).astype(k_ref.dtype)
    lf_ref[...] = jnp.log2(lb + one_lb * (0.5 + th))


class _Cast(NamedTuple):
    src: jax.Array
    col0: int
    width: int


def _cast_blocks(cast, steps):
    cw = math.gcd(cast.col0, cast.width, CAST_COLS)
    ncb = cast.width // cw
    return (cast.src.shape[0] * ncb) // steps, cw, ncb


def _cast_rides(cast, steps):
    rb, _, ncb = _cast_blocks(cast, steps)
    rows = cast.src.shape[0]
    return rb * steps == rows * ncb and rb % BF16_SUBLANES == 0 and rows % rb == 0


def _split_casts(casts, steps):
    return [c for c in casts if _cast_rides(c, steps)]


def _cast_results(casts, steps, ridden):
    ridden = list(ridden)
    return [ridden.pop(0) if _cast_rides(c, steps) else c.src[:, c.col0:c.col0 + c.width].astype(BF16)
            for c in casts]


def _cast_specs(casts, steps, step_of):
    in_specs, out_specs, out_shapes = [], [], []
    for cast in casts:
        src, col0, width = cast
        rows = src.shape[0]
        rb, cw, ncb = _cast_blocks(cast, steps)
        coff = col0 // cw
        in_specs.append(pl.BlockSpec(
            (rb, cw), lambda i, j, ncb=ncb, coff=coff: (step_of(i, j) // ncb, coff + step_of(i, j) % ncb)))
        out_specs.append(pl.BlockSpec(
            (rb, cw), lambda i, j, ncb=ncb: (step_of(i, j) // ncb, step_of(i, j) % ncb)))
        out_shapes.append(jax.ShapeDtypeStruct((rows, width), BF16))
    return in_specs, out_specs, out_shapes


def _do_casts(src_refs, dst_refs):
    for s_ref, d_ref in zip(src_refs, dst_refs, strict=True):
        d_ref[...] = s_ref[...].astype(BF16)


def _proj_kernel(epilogue, n_extra, n_cast, x_ref, w_ref, *refs):
    extras, refs = refs[:n_extra], refs[n_extra:]
    cast_src, refs = refs[:n_cast], refs[n_cast:]
    outs, cast_dst = refs[:len(refs) - n_cast], refs[len(refs) - n_cast:]
    acc = jnp.dot(x_ref[...], w_ref[...], preferred_element_type=F32)
    epilogue(acc, *extras, *outs)
    _do_casts(cast_src, cast_dst)


def _in_proj(h, w, epilogue, out_dtypes, extra=(), casts=(), name="in_proj"):
    m, d = h.shape
    width = w.shape[1]
    tm = min(MXU_TILE, m)
    tn = min(MXU_TILE, width)
    n_i, n_j = m // tm, width // tn
    out_blk = pl.BlockSpec((tm, tn), lambda i, j: (i, j))
    riding = _split_casts(casts, n_i * n_j)
    c_in, c_out, c_shapes = _cast_specs(riding, n_i * n_j, lambda i, j: i * n_j + j)
    res = pl.pallas_call(
        functools.partial(_proj_kernel, epilogue, len(extra), len(riding)),
        grid=(n_i, n_j),
        in_specs=[pl.BlockSpec((tm, d), lambda i, j: (i, 0)),
                  pl.BlockSpec((d, tn), lambda i, j: (0, j))]
                 + [pl.BlockSpec((1, tn), lambda i, j: (0, j)) for _ in extra] + c_in,
        out_specs=[out_blk for _ in out_dtypes] + c_out,
        out_shape=[jax.ShapeDtypeStruct((m, width), dt) for dt in out_dtypes] + c_shapes,
        compiler_params=_cparams("parallel", "arbitrary"),
        name=name,
    )(h, w, *extra, *[c.src for c in riding])
    n_out = len(out_dtypes)
    return list(res[:n_out]) + _cast_results(casts, n_i * n_j, res[n_out:])


def _mid_rows(b, m):
    c, w = b.shape
    assert m in (1, 2, 4)
    if m == 1:
        odd = (lax.broadcasted_iota(jnp.int32, (c, w), 0) & 1) == 1
        return jnp.where(odd, pltpu.roll(b, 1, 0), b)
    b3 = b.reshape(c // 8, 8, w)
    if m == 4:
        return jnp.broadcast_to(b3[:, 3:4, :], b3.shape).reshape(c, w)
    low = lax.broadcasted_iota(jnp.int32, b3.shape, 1) < 4
    return jnp.where(low, jnp.broadcast_to(b3[:, 1:2, :], b3.shape),
                     jnp.broadcast_to(b3[:, 5:6, :], b3.shape)).reshape(c, w)


def _block_sums(tri, lf, lb):
    k = lf.shape[1]
    parts = []
    for a in (lf, lb):
        hi = a.astype(BF16)
        parts += [hi, (a - hi.astype(F32)).astype(BF16)]
    cs = jnp.dot(tri, jnp.concatenate(parts, axis=1), preferred_element_type=F32)
    return jnp.concatenate([cs[:, :k] + cs[:, k:2 * k], cs[:, 2 * k:3 * k] + cs[:, 3 * k:]], axis=1)


def _state_terms(v, kf, kb, bcat, lb):
    c, k = lb.shape
    tot = bcat[c - 1:c]
    ks = jnp.concatenate([kf * jnp.exp2(tot[:, :k] - bcat[:, :k]),
                          kb * jnp.exp2(bcat[:, k:] - lb)], axis=1)
    return pl.dot(v, ks.astype(BF16), trans_a=True), jnp.exp2(tot)


def _level_products(q, kf, kb, bcat, lb, par_ref):
    c, k = lb.shape
    xf, xb = bcat[:, :k], bcat[:, k:] - lb
    prods = []
    m, level = 1, 0
    while m < c:
        if m % 8 == 0:
            split = lambda a: a.reshape(c // (2 * m), 2, m, a.shape[-1])
            join = lambda even, odd: jnp.concatenate([even, odd], axis=1).reshape(c, k)
            xf4, xb4, kf4, kb4 = split(xf), split(xb), split(kf), split(kb)
            mid = split(bcat)[:, 0:1, m - 1:m, :]
            mf, mb = mid[..., :k], mid[..., k:]
            eq = join(mb - xb4[:, 0:1], xf4[:, 1:2] - mf)
            ek = join(mf - xf4[:, 0:1], xb4[:, 1:2] - mb)
            ksel = join(kf4[:, 0:1], kb4[:, 1:2])
        else:
            mid = _mid_rows(bcat, m)
            df = xf - mid[:, :k]
            db = mid[:, k:] - xb
            eq = jnp.minimum(df, db)
            ek = -jnp.maximum(df, db)
            ksel = jnp.where(par_ref[level] > 0, kb, kf)
        prods.append(pl.dot(q * jnp.exp2(eq), ksel * jnp.exp2(ek), trans_b=True))
        m *= 2
        level += 1
    prods.append(pl.dot(q, kf + kb, trans_b=True))
    return prods


def _assemble(prods):
    c = prods[0].shape[0]
    x = (lax.broadcasted_iota(jnp.int32, (c, c), 0) ^ lax.broadcasted_iota(jnp.int32, (c, c), 1))
    att = prods[-2]
    m = c // 2
    for a in reversed(prods[:-2]):
        att = jnp.where(x < m, a, att)
        m //= 2
    return jnp.where(x < 1, prods[-1], att).astype(BF16)


def _entry_factors(q, bcat, lb):
    c, k = lb.shape
    xb = bcat[:, k:] - lb
    return jnp.concatenate([q * jnp.exp2(bcat[:, :k]),
                            q * jnp.exp2(bcat[c - 1:c, k:] - xb)], axis=1).astype(BF16)


def _scan_kernel(nx, nc,
                 qx, vx, kfx, kbx, lfx, lbx, gx, qc, vc, kfc, kbc, lfc, lbc, gain_ref, tri_ref, par_ref,
                 y_ref, oi_ref, qi_ref, u_ref, d_ref, s_ref, b_ref):
    c = SCAN_BLOCK
    k = HEAD_DIM
    f32 = lambda ref, rows: ref[rows, :].astype(F32)

    for j in range(nc):
        rows = pl.ds(j * c, c)
        bcat = _block_sums(tri_ref[...], lfc[rows, :], lbc[rows, :])
        u_t, dec = _state_terms(vc[rows, :], f32(kfc, rows), f32(kbc, rows), bcat, lbc[rows, :])
        u_ref[j] = u_t
        d_ref[j] = dec

    group = math.gcd(nx, 4)
    n_groups = nx // group

    def sums(g, slot):
        for i in range(group):
            rows = pl.ds(pl.multiple_of((g * group + i) * c, c), c)
            b_ref[slot, pl.ds(i * c, c), :] = _block_sums(tri_ref[...], lfx[rows, :], lbx[rows, :])

    sums(0, 0)

    def do_group(p, slot):
        sums(jnp.minimum(p + 1, n_groups - 1), 1 - slot)
        rows = [pl.ds(pl.multiple_of((p * group + i) * c, c), c) for i in range(group)]
        bcats = [b_ref[slot, pl.ds(i * c, c), :] for i in range(group)]
        qkk = [(f32(qx, rw), f32(kfx, rw), f32(kbx, rw)) for rw in rows]
        prods = [_level_products(q, kf, kb, bcat, lbx[rw, :], par_ref)
                 for (q, kf, kb), rw, bcat in zip(qkk, rows, bcats)]
        for i, ((_, kf, kb), rw, bcat) in enumerate(zip(qkk, rows, bcats)):
            u_t, dec = _state_terms(vx[rw, :], kf, kb, bcat, lbx[rw, :])
            u_ref[nc + p * group + i] = u_t
            d_ref[nc + p * group + i] = dec
        outs = [jnp.dot(_assemble(pr), vx[rw, :], preferred_element_type=F32) for pr, rw in zip(prods, rows)]
        for (q, _, _), rw, bcat in zip(qkk, rows, bcats):
            qi_ref[rw, :] = _entry_factors(q, bcat, lbx[rw, :])
        for o, rw in zip(outs, rows):
            oi_ref[rw, :] = o

    def pair_body(pp, carry):
        do_group(2 * pp, 0)
        do_group(2 * pp + 1, 1)
        return carry

    lax.fori_loop(0, n_groups // 2, pair_body, 0)
    if n_groups % 2:
        do_group(n_groups - 1, 0)

    sf = jnp.zeros((k, k), F32)
    sb = jnp.zeros((k, k), F32)
    for j in range(nc):
        sf = d_ref[j][:, :k] * sf + u_ref[j][:, :k]
        jb = nc - 1 - j
        sb = d_ref[jb][:, k:] * sb + u_ref[jb][:, k:]

    def sweep_body(i, carry):
        sf, sb = carry
        s_ref[i, :, :k] = sf.astype(BF16)
        sf = d_ref[nc + i][:, :k] * sf + u_ref[nc + i][:, :k]
        j = nx - 1 - i
        s_ref[j, :, k:] = sb.astype(BF16)
        sb = d_ref[nc + j][:, k:] * sb + u_ref[nc + j][:, k:]
        return sf, sb

    lax.fori_loop(0, nx, sweep_body, (sf, sb))

    gain = gain_ref[...]

    def out_body(n, carry):
        rows = pl.ds(pl.multiple_of(n * c, c), c)
        o = oi_ref[rows, :] + pl.dot(qi_ref[rows, :], s_ref[n], trans_b=True)
        o = o * lax.rsqrt(jnp.mean(o * o, axis=-1, keepdims=True) + LN_EPS) * gain
        y_ref[rows, :] = (o * gx[rows, :].astype(F32)).astype(BF16)
        return carry

    lax.fori_loop(0, nx, out_body, 0, unroll=math.gcd(nx, 8))


def _hgrn2_scan(px, pc, gain, b, t, tc):
    qx, vx, kkx, lfx, gx = px
    qc, vc, kkc, lfc = pc
    k = HEAD_DIM
    heads = qx.shape[1] // k
    nx, nc = t // SCAN_BLOCK, tc // SCAN_BLOCK
    n_levels = SCAN_BLOCK.bit_length() - 1
    bits = (jnp.arange(SCAN_BLOCK)[None, :] >> jnp.arange(n_levels)[:, None]) & 1
    parity = jnp.broadcast_to(bits[:, :, None], (n_levels, SCAN_BLOCK, k)).astype(F32)
    fwd = lambda rows: pl.BlockSpec((rows, k), lambda i, h: (i, h))
    bwd = lambda rows: pl.BlockSpec((rows, k), lambda i, h: (i, h + heads))
    return pl.pallas_call(
        functools.partial(_scan_kernel, nx, nc),
        grid=(b, heads),
        in_specs=[fwd(t), fwd(t), fwd(t), bwd(t), fwd(t), bwd(t), fwd(t),
                  fwd(tc), fwd(tc), fwd(tc), bwd(tc), fwd(tc), bwd(tc),
                  pl.BlockSpec((1, k), lambda i, h: (0, 0)),
                  pl.BlockSpec((SCAN_BLOCK, SCAN_BLOCK), lambda i, h: (0, 0)),
                  pl.BlockSpec(parity.shape, lambda i, h: (0, 0, 0))],
        out_specs=fwd(t),
        out_shape=jax.ShapeDtypeStruct((b * t, heads * k), BF16),
        scratch_shapes=[pltpu.VMEM((t, k), F32),
                        pltpu.VMEM((t, 2 * k), BF16),
                        pltpu.VMEM((nx + nc, k, 2 * k), F32),
                        pltpu.VMEM((nx + nc, 1, 2 * k), F32),
                        pltpu.VMEM((nx, k, 2 * k), BF16),
                        pltpu.VMEM((2, math.gcd(nx, 4) * SCAN_BLOCK, 2 * k), F32)],
        compiler_params=_cparams("parallel", "parallel"),
        name="hgrn2_scan",
    )(qx, vx, kkx, kkx, lfx, lfx, gx, qc, vc, kkc, kkc, lfc, lfc, gain,
      jnp.tril(jnp.ones((SCAN_BLOCK, SCAN_BLOCK), BF16)), parity)


def _mix_kernel(n_groups, chunk, u_ref, v_ref, g_ref, b_ref, ws_ref, bias_ref, y_ref, vn_ref):
    vn_ref[...] = (_ln_rows(v_ref[...].astype(F32)) * g_ref[...] + b_ref[...]).astype(BF16)
    gc = v_ref.shape[1] // n_groups
    for cidx in range(v_ref.shape[0] // chunk):
        rows = pl.ds(cidx * chunk, chunk)
        for g in range(n_groups):
            cols = pl.ds(g * gc, gc)
            mixed = jnp.dot(ws_ref[g], vn_ref[rows, cols], preferred_element_type=F32)
            y_ref[rows, cols] = (u_ref[rows, cols].astype(F32) * (mixed + bias_ref[:, cols])).astype(BF16)


def _chunk_mix(uv, v_g, v_b, w_s, bias, rows_per_step):
    m = uv.shape[0]
    bw = uv.shape[1] // 2
    groups, chunk, _ = w_s.shape
    tm = rows_per_step
    return pl.pallas_call(
        functools.partial(_mix_kernel, groups, chunk),
        grid=(m // tm,),
        in_specs=[pl.BlockSpec((tm, bw), lambda i: (i, 0)),
                  pl.BlockSpec((tm, bw), lambda i: (i, 1)),
                  pl.BlockSpec((1, bw), lambda i: (0, 0)),
                  pl.BlockSpec((1, bw), lambda i: (0, 0)),
                  pl.BlockSpec((groups, chunk, chunk), lambda i: (0, 0, 0)),
                  pl.BlockSpec((chunk, bw), lambda i: (0, 0))],
        out_specs=pl.BlockSpec((tm, bw), lambda i: (i, 0)),
        out_shape=jax.ShapeDtypeStruct((m, bw), BF16),
        scratch_shapes=[pltpu.VMEM((tm, bw), BF16)],
        compiler_params=_cparams("parallel"),
        name="chunk_mix",
    )(uv, uv, v_g, v_b, w_s, bias)


def _merge_kernel(n_cast, ya_ref, yb_ref, wa_ref, wb_ref, ga_ref, gb_ref, *refs):
    cast_src, o_ref, cast_dst = refs[:n_cast], refs[n_cast], refs[n_cast + 1:]
    a = jnp.dot(ya_ref[...], wa_ref[...], preferred_element_type=F32)
    bq = jnp.dot(yb_ref[...], wb_ref[...], preferred_element_type=F32)
    o_ref[...] = (ga_ref[...].astype(F32) * a + gb_ref[...].astype(F32) * bq).astype(BF16)
    _do_casts(cast_src, cast_dst)


def _merge(ya, yb, wa, wb, gates, casts=()):
    m, ka = ya.shape
    kb = yb.shape[1]
    d = wa.shape[1]
    tm, tn = min(MXU_TILE, m), min(MXU_TILE, d)
    nd = d // tn
    steps = (m // tm) * nd
    riding = _split_casts(casts, steps)
    c_in, c_out, c_shapes = _cast_specs(riding, steps, lambda i, j: i * nd + j)
    res = pl.pallas_call(
        functools.partial(_merge_kernel, len(riding)),
        grid=(m // tm, nd),
        in_specs=[pl.BlockSpec((tm, ka), lambda i, j: (i, 0)),
                  pl.BlockSpec((tm, kb), lambda i, j: (i, 0)),
                  pl.BlockSpec((ka, tn), lambda i, j: (0, j)),
                  pl.BlockSpec((kb, tn), lambda i, j: (0, j)),
                  pl.BlockSpec((tm, tn), lambda i, j: (i, j)),
                  pl.BlockSpec((tm, tn), lambda i, j: (i, j + nd))] + c_in,
        out_specs=[pl.BlockSpec((tm, tn), lambda i, j: (i, j))] + c_out,
        out_shape=[jax.ShapeDtypeStruct((m, d), BF16)] + c_shapes,
        compiler_params=_cparams("parallel", "arbitrary"),
        name="branch_merge",
    )(ya, yb, wa, wb, gates, gates, *[c.src for c in riding])
    return [res[0]] + _cast_results(casts, steps, res[1:])


def _resid_kernel(alpha, x_ref, w_ref, r_ref, gate_ref, o_ref):
    y = jnp.dot(x_ref[...], w_ref[...], preferred_element_type=F32)
    o_ref[...] = alpha * r_ref[...] + gate_ref[...] * y


def _out_proj_resid(xin, w, resid, mod3, gate_idx, alpha, t):
    m, kdim = xin.shape
    d = w.shape[1]
    tm, tn = min(MXU_TILE, m, t), min(MXU_TILE, d)
    per_batch = t // tm
    nd = d // tn
    return pl.pallas_call(
        functools.partial(_resid_kernel, alpha),
        grid=(m // tm, nd),
        in_specs=[pl.BlockSpec((tm, kdim), lambda i, j: (i, 0)),
                  pl.BlockSpec((kdim, tn), lambda i, j: (0, j)),
                  pl.BlockSpec((tm, tn), lambda i, j: (i, j)),
                  pl.BlockSpec((None, 1, tn), lambda i, j: (i // per_batch, 0, gate_idx * nd + j))],
        out_specs=pl.BlockSpec((tm, tn), lambda i, j: (i, j)),
        out_shape=jax.ShapeDtypeStruct((m, d), F32),
        compiler_params=_cparams("parallel", "arbitrary"),
        name="out_proj_resid",
    )(xin, w, resid, mod3)


def _ln_ffn_up_kernel(n_cast, p_ref, g_ref, b_ref, sh_ref, sc_ref, w_ref, *refs):
    cast_src, refs = refs[:n_cast], refs[n_cast:]
    x1_ref, act_ref = refs[0], refs[1]
    cast_dst, h_ref = refs[2:2 + n_cast], refs[2 + n_cast]
    i, j = pl.program_id(0), pl.program_id(1)
    rows = p_ref.shape[0]

    def prepare():
        x = _ln_rows(p_ref[...]) * g_ref[...] + b_ref[...]
        x1_ref[...] = x
        h = _ln_rows(x) * (1.0 + sc_ref[...]) + sh_ref[...]
        h_ref[i % 2, pl.ds(pl.multiple_of(j * rows, rows), rows), :] = h.astype(BF16)
        _do_casts(cast_src, cast_dst)

    @pl.when(i == 0)
    def _():
        prepare()

    @pl.when(i > 0)
    def _():
        y = jnp.dot(h_ref[(i - 1) % 2], w_ref[...], preferred_element_type=F32)
        act_ref[...] = jnp.square(jnp.maximum(y, 0.0)).astype(BF16)
        prepare()


def _ln_ffn_up(pre, g, bvec, mod3, shift_idx, scale_idx, t, w, casts=()):
    m, d = pre.shape
    n = w.shape[1]
    tm, tn = min(MXU_TILE, m, t), min(MXU_TILE, n)
    n_i, n_j = m // tm, n // tn
    rows = tm // n_j
    assert rows * n_j == tm and rows % BF16_SUBLANES == 0
    per_batch = t // tm
    tile = lambda i: jnp.minimum(i, n_i - 1)
    prev = lambda i: jnp.maximum(i - 1, 0)
    steps = n_i * n_j
    slab = pl.BlockSpec((rows, d), lambda i, j: (jnp.minimum(i * n_j + j, steps - 1), 0))
    vec = pl.BlockSpec((1, d), lambda i, j: (0, 0))
    riding = _split_casts(casts, steps)
    c_in, c_out, c_shapes = _cast_specs(riding, steps, lambda i, j: jnp.where(i > 0, (i - 1) * n_j + j, 0))
    res = pl.pallas_call(
        functools.partial(_ln_ffn_up_kernel, len(riding)),
        grid=(n_i + 1, n_j),
        in_specs=[slab, vec, vec,
                  pl.BlockSpec((None, 1, d), lambda i, j: (tile(i) // per_batch, 0, shift_idx)),
                  pl.BlockSpec((None, 1, d), lambda i, j: (tile(i) // per_batch, 0, scale_idx)),
                  pl.BlockSpec((d, tn), lambda i, j: (0, j))] + c_in,
        out_specs=[slab, pl.BlockSpec((tm, tn), lambda i, j: (prev(i), jnp.where(i > 0, j, 0)))] + c_out,
        out_shape=[jax.ShapeDtypeStruct((m, d), F32), jax.ShapeDtypeStruct((m, n), BF16)] + c_shapes,
        scratch_shapes=[pltpu.VMEM((2, tm, d), BF16)],
        compiler_params=_cparams("arbitrary", "arbitrary"),
        name="ln_ffn_up",
    )(pre, g, bvec, mod3, mod3, w, *[c.src for c in riding])
    return [res[0], res[1]] + _cast_results(casts, steps, res[2:])


def _ln_kernel(p_ref, g_ref, b_ref, o_ref):
    o_ref[...] = _ln_rows(p_ref[...]) * g_ref[...] + b_ref[...]


def _ln(pre, g, bvec, tm):
    m, d = pre.shape
    row = pl.BlockSpec((tm, d), lambda i: (i, 0))
    vec = pl.BlockSpec((1, d), lambda i: (0, 0))
    return pl.pallas_call(
        _ln_kernel,
        grid=(m // tm,),
        in_specs=[row, vec, vec],
        out_specs=row,
        out_shape=jax.ShapeDtypeStruct((m, d), F32),
        compiler_params=_cparams("parallel"),
        name="ln_out",
    )(pre, g, bvec)


def _ffn2_kernel(alpha, a_ref, w_ref, r_ref, gate_ref, o_ref):
    kk = pl.program_id(2)

    @pl.when(kk == 0)
    def _():
        o_ref[...] = jnp.dot(a_ref[...], w_ref[...], preferred_element_type=F32)

    @pl.when(kk > 0)
    def _():
        o_ref[...] += jnp.dot(a_ref[...], w_ref[...], preferred_element_type=F32)

    @pl.when(kk == pl.num_programs(2) - 1)
    def _():
        o_ref[...] = alpha * r_ref[...] + gate_ref[...] * o_ref[...]


def _ffn2_resid(a, w, resid, mod3, gate_idx, alpha, t):
    m, f = a.shape
    d = w.shape[1]
    tm, tn, tk = min(MXU_TILE, m, t), min(MXU_TILE, d), min(4 * MXU_TILE, f)
    per_batch = t // tm
    nd = d // tn
    return pl.pallas_call(
        functools.partial(_ffn2_kernel, alpha),
        grid=(m // tm, nd, f // tk),
        in_specs=[pl.BlockSpec((tm, tk), lambda i, j, kk: (i, kk)),
                  pl.BlockSpec((tk, tn), lambda i, j, kk: (kk, j)),
                  pl.BlockSpec((tm, tn), lambda i, j, kk: (i, j)),
                  pl.BlockSpec((None, 1, tn), lambda i, j, kk: (i // per_batch, 0, gate_idx * nd + j))],
        out_specs=pl.BlockSpec((tm, tn), lambda i, j, kk: (i, j)),
        out_shape=jax.ShapeDtypeStruct((m, d), F32),
        compiler_params=_cparams("parallel", "parallel", "arbitrary"),
        name="ffn_down_resid",
    )(a, w, resid, mod3)


def _pos_tables(rows, cols, dim):
    quarter = dim // 4
    omega = 1.0 / (POS_BASE ** (jnp.arange(quarter, dtype=F32) / quarter))

    def emb(p):
        a = p[:, None] * omega[None, :]
        return jnp.concatenate([jnp.sin(a), jnp.cos(a)], axis=-1)

    return (emb(jnp.arange(rows, dtype=F32))[:, None, :], emb(jnp.arange(cols, dtype=F32))[None, :, :])


def kernel(x, c, ctx, c_ctx, w_ada, b_ada, w_in, lb_logits, a_norm_g, w_proj_a, v_norm_g, v_norm_b,
           w_s, b_s, w_proj_b, w_out, ln1_g, ln1_b, w_ff1, w_ff2, ln2_g, ln2_b):
    depth = w_ada.shape[0]
    assert depth == 1, "only the single-layer block is implemented"
    b, t, d = x.shape
    tc = ctx.shape[1]
    a_fdim = lb_logits.shape[-1]
    a_width = w_proj_a.shape[1]
    b_width = w_proj_b.shape[1]
    assert a_fdim == a_width and a_norm_g.shape[-1] == HEAD_DIM
    assert t % GRID_W == 0 and t % SCAN_BLOCK == 0 and tc % SCAN_BLOCK == 0
    assert w_s.shape[2] == SCAN_BLOCK or t % w_s.shape[2] == 0
    alpha = (2.0 * depth) ** 0.25
    m = b * t

    sizes = (a_fdim, a_width, a_fdim, a_fdim, a_width, b_width, b_width, d, d)
    offs = [0]
    for s in sizes:
        offs.append(offs[-1] + s)
    assert offs[-1] == w_in.shape[-1]

    w_in0 = w_in[0]
    seg = lambda first, last: _Cast(w_in0, offs[first], offs[last + 1] - offs[first])
    wq_b = w_in0[:, :offs[1]].astype(BF16)
    ws_b = w_s[0].astype(BF16)
    lb_all = jnp.cumsum(jax.nn.softmax(lb_logits.astype(F32), axis=1), axis=1)
    lb_row = jnp.concatenate([lb_all[0, 0], lb_all[1, 0]])[None, :]
    mix_bias = jnp.repeat(b_s[0].T, b_width // w_s.shape[1], axis=1)
    row_tab, col_tab = _pos_tables(t // GRID_W, GRID_W, d)

    n_rows = -(-(b + 1) // 8) * 8
    cc = jnp.concatenate([c, c_ctx[None, :], jnp.zeros((n_rows - b - 1, d), F32)], axis=0)
    mod = _modulation(cc, w_ada[0], b_ada[0][None, :])
    mod3 = mod.reshape(n_rows, 1, N_MOD * d)

    xp, h = _adaln_pos(x, row_tab, col_tab, mod3, rows_per_step=min(4, t // GRID_W))
    hc = _adaln_ctx(ctx.reshape(b * tc, d), mod3, b, tm=min(256, b * tc))

    scale = HEAD_DIM ** -0.5
    ep_q = functools.partial(_ep_silu, scale)
    ep_g = functools.partial(_ep_silu, 1.0)

    qx, wi_b = _in_proj(h, wq_b, ep_q, [BF16], casts=(seg(1, 1),), name="in_proj_q")
    vx, wf_b = _in_proj(h, wi_b, _ep_cast, [BF16], casts=(seg(2, 3),), name="in_proj_i")
    kkx, lfx, wg_b = _in_proj(h, wf_b, _ep_decay, [BF16, F32], extra=(lb_row,), casts=(seg(4, 4),),
                              name="in_proj_f")
    gx, wuv_b = _in_proj(h, wg_b, ep_g, [BF16], casts=(seg(5, 6),), name="in_proj_g")
    uv, wgate_b = _in_proj(h, wuv_b, _ep_cast, [BF16], casts=(seg(7, 8),), name="in_proj_uv")
    gates, wa_b, wb_b, w1_b = _in_proj(
        h, wgate_b, _ep_sigmoid, [BF16],
        casts=(_Cast(w_proj_a[0], 0, d), _Cast(w_proj_b[0], 0, d), _Cast(w_ff1[0], 0, w_ff1.shape[-1])),
        name="in_proj_gates")

    (qc,) = _in_proj(hc, wq_b, ep_q, [BF16], name="in_proj_q_ctx")
    (vc,) = _in_proj(hc, wi_b, _ep_cast, [BF16], name="in_proj_i_ctx")
    kkc, lfc = _in_proj(hc, wf_b, _ep_decay, [BF16, F32], extra=(lb_row,), name="in_proj_f_ctx")

    ya = _hgrn2_scan((qx, vx, kkx, lfx, gx), (qc, vc, kkc, lfc), a_norm_g[0][None, :], b, t, tc)
    yb = _chunk_mix(uv, v_norm_g[0][None, :], v_norm_b[0][None, :], ws_b, mix_bias,
                    rows_per_step=min(2 * w_s.shape[2], t))

    merged, wo_b = _merge(ya, yb, wa_b, wb_b, gates, casts=(_Cast(w_out[0], 0, d),))
    pre1 = _out_proj_resid(merged, wo_b, xp, mod3, 2, alpha, t)
    x1, act, w2_b = _ln_ffn_up(pre1, ln1_g[0][None, :], ln1_b[0][None, :], mod3, 3, 4, t, w1_b,
                               casts=(_Cast(w_ff2[0], 0, d),))
    pre2 = _ffn2_resid(act, w2_b, x1, mod3, 5, alpha, t)
    out = _ln(pre2, ln2_g[0][None, :], ln2_b[0][None, :], tm=min(256, m))
    return out.reshape(b, t, d)
```

```python
import functools
import math
from typing import NamedTuple

import jax
import jax.numpy as jnp
from jax import lax
from jax.experimental import pallas as pl
from jax.experimental.pallas import tpu as pltpu

F32 = jnp.float32
BF16 = jnp.bfloat16

GRID_W = 64
LN_EPS = 1e-6
POS_BASE = 10000.0
N_MOD = 6
HEAD_DIM = 128
SCAN_BLOCK = 128
MXU_TILE = 1024
CAST_COLS = 4096
BF16_SUBLANES = 16
VMEM_LIMIT = 56 * 1024 * 1024
VMEM_LIMIT_WIDE = 60 * 1024 * 1024


def _cparams(*sem, vmem=VMEM_LIMIT):
    return pltpu.CompilerParams(dimension_semantics=sem, vmem_limit_bytes=vmem)


def _sigmoid(x):
    return 0.5 * jnp.tanh(0.5 * x) + 0.5


def _ln_rows(x):
    mu = jnp.mean(x, axis=-1, keepdims=True)
    xc = x - mu
    return xc * lax.rsqrt(jnp.mean(xc * xc, axis=-1, keepdims=True) + LN_EPS)


def _mod_kernel(c_ref, w_ref, b_ref, o_ref):
    a = c_ref[...]
    a = a * _sigmoid(a)
    o_ref[...] = jnp.dot(a.astype(BF16), w_ref[...].astype(BF16),
                         preferred_element_type=F32) + b_ref[...]


def _modulation(cc, w_ada, b_ada):
    rows, d = cc.shape
    n = w_ada.shape[1]
    tn = min(512, n)
    return pl.pallas_call(
        _mod_kernel,
        grid=(n // tn,),
        in_specs=[pl.BlockSpec((rows, d), lambda j: (0, 0)),
                  pl.BlockSpec((d, tn), lambda j: (0, j)),
                  pl.BlockSpec((1, tn), lambda j: (0, j))],
        out_specs=pl.BlockSpec((rows, tn), lambda j: (0, j)),
        out_shape=jax.ShapeDtypeStruct((rows, n), F32),
        compiler_params=_cparams("arbitrary"),
        name="modulation",
    )(cc, w_ada, b_ada)


def _adaln_pos_kernel(x_ref, rt_ref, ct_ref, sh_ref, sc_ref, h_ref):
    half = rt_ref.shape[-1]
    d = 2 * half
    xa = x_ref[:, :, :half] + rt_ref[...]
    xb = x_ref[:, :, half:] + ct_ref[...]
    mu = (jnp.sum(xa, axis=-1, keepdims=True) + jnp.sum(xb, axis=-1, keepdims=True)) * (1.0 / d)
    xa = xa - mu
    xb = xb - mu
    var = (jnp.sum(xa * xa, axis=-1, keepdims=True) + jnp.sum(xb * xb, axis=-1, keepdims=True)) * (1.0 / d)
    inv = lax.rsqrt(var + LN_EPS)
    sh = sh_ref[...]
    sc = 1.0 + sc_ref[...]
    h_ref[:, :, :half] = (xa * inv * sc[:, :half] + sh[:, :half]).astype(BF16)
    h_ref[:, :, half:] = (xb * inv * sc[:, half:] + sh[:, half:]).astype(BF16)


def _adaln_pos(x, row_tab, col_tab, mod3, rows_per_step):
    b, t, d = x.shape
    gr = t // GRID_W
    r = rows_per_step
    x4 = x.reshape(b, gr, GRID_W, d)
    blk = pl.BlockSpec((None, r, GRID_W, d), lambda i, j: (i, j, 0, 0))
    h = pl.pallas_call(
        _adaln_pos_kernel,
        grid=(b, gr // r),
        in_specs=[blk,
                  pl.BlockSpec((r, 1, d // 2), lambda i, j: (j, 0, 0)),
                  pl.BlockSpec((1, GRID_W, d // 2), lambda i, j: (0, 0, 0)),
                  pl.BlockSpec((None, 1, d), lambda i, j: (i, 0, 0)),
                  pl.BlockSpec((None, 1, d), lambda i, j: (i, 0, 1))],
        out_specs=blk,
        out_shape=jax.ShapeDtypeStruct(x4.shape, BF16),
        compiler_params=_cparams("parallel", "parallel"),
        name="adaln_pos",
    )(x4, row_tab, col_tab, mod3, mod3)
    return h.reshape(b * t, d)


def _adaln_rows_kernel(x_ref, sh_ref, sc_ref, h_ref):
    y = _ln_rows(x_ref[...])
    h_ref[...] = (y * (1.0 + sc_ref[...]) + sh_ref[...]).astype(BF16)


def _adaln_ctx(xc, mod3, ctx_row, tm):
    m, d = xc.shape
    return pl.pallas_call(
        _adaln_rows_kernel,
        grid=(m // tm,),
        in_specs=[pl.BlockSpec((tm, d), lambda i: (i, 0)),
                  pl.BlockSpec((None, 1, d), lambda i: (ctx_row, 0, 0)),
                  pl.BlockSpec((None, 1, d), lambda i: (ctx_row, 0, 1))],
        out_specs=pl.BlockSpec((tm, d), lambda i: (i, 0)),
        out_shape=jax.ShapeDtypeStruct((m, d), BF16),
        compiler_params=_cparams("parallel"),
        name="adaln_ctx",
    )(xc, mod3, mod3)


def _ep_silu(scale, acc, o_ref):
    o_ref[...] = (acc * _sigmoid(acc) * scale).astype(o_ref.dtype)


def _ep_cast(acc, o_ref):
    o_ref[...] = acc.astype(o_ref.dtype)


def _ep_sigmoid(acc, o_ref):
    o_ref[...] = _sigmoid(acc).astype(o_ref.dtype)


def _ep_relu2(acc, o_ref):
    o_ref[...] = jnp.square(jnp.maximum(acc, 0.0)).astype(o_ref.dtype)


def _ep_decay(acc, lb_ref, k_ref, lf_ref):
    lb = lb_ref[...]
    c = 0.5 - 0.5 * lb
    ct = c * jnp.tanh(0.5 * acc)
    k_ref[...] = (c - ct).astype(k_ref.dtype)
    lf_ref[...] = jnp.log2((0.5 + 0.5 * lb) + ct)


class _Cast(NamedTuple):
    src: jax.Array
    col0: int
    width: int


def _cast_blocks(cast, steps):
    cw = math.gcd(cast.col0, cast.width, CAST_COLS)
    ncb = cast.width // cw
    return (cast.src.shape[0] * ncb) // steps, cw, ncb


def _cast_rides(cast, steps):
    rb, _, ncb = _cast_blocks(cast, steps)
    rows = cast.src.shape[0]
    return rb * steps == rows * ncb and rb % BF16_SUBLANES == 0 and rows % rb == 0


def _split_casts(casts, steps):
    return [c for c in casts if _cast_rides(c, steps)]


def _cast_results(casts, steps, ridden):
    ridden = list(ridden)
    return [ridden.pop(0) if _cast_rides(c, steps) else c.src[:, c.col0:c.col0 + c.width].astype(BF16)
            for c in casts]


def _cast_specs(casts, steps, step_of):
    in_specs, out_specs, out_shapes = [], [], []
    for cast in casts:
        src, col0, width = cast
        rows = src.shape[0]
        rb, cw, ncb = _cast_blocks(cast, steps)
        coff = col0 // cw
        in_specs.append(pl.BlockSpec(
            (rb, cw), lambda i, j, ncb=ncb, coff=coff: (step_of(i, j) // ncb, coff + step_of(i, j) % ncb)))
        out_specs.append(pl.BlockSpec(
            (rb, cw), lambda i, j, ncb=ncb: (step_of(i, j) // ncb, step_of(i, j) % ncb)))
        out_shapes.append(jax.ShapeDtypeStruct((rows, width), BF16))
    return in_specs, out_specs, out_shapes


def _do_casts(src_refs, dst_refs):
    for s_ref, d_ref in zip(src_refs, dst_refs, strict=True):
        d_ref[...] = s_ref[...].astype(BF16)


def _proj_kernel(epilogue, n_extra, n_cast, x_ref, w_ref, *refs):
    extras, refs = refs[:n_extra], refs[n_extra:]
    cast_src, refs = refs[:n_cast], refs[n_cast:]
    outs, cast_dst = refs[:len(refs) - n_cast], refs[len(refs) - n_cast:]
    acc = jnp.dot(x_ref[...], w_ref[...], preferred_element_type=F32)
    epilogue(acc, *extras, *outs)
    _do_casts(cast_src, cast_dst)


def _in_proj(h, w, epilogue, out_dtypes, extra=(), casts=(), name="in_proj"):
    m, d = h.shape
    width = w.shape[1]
    tm = min(MXU_TILE, m)
    tn = min(MXU_TILE, width)
    n_i, n_j = m // tm, width // tn
    out_blk = pl.BlockSpec((tm, tn), lambda i, j: (i, j))
    riding = _split_casts(casts, n_i * n_j)
    c_in, c_out, c_shapes = _cast_specs(riding, n_i * n_j, lambda i, j: i * n_j + j)
    res = pl.pallas_call(
        functools.partial(_proj_kernel, epilogue, len(extra), len(riding)),
        grid=(n_i, n_j),
        in_specs=[pl.BlockSpec((tm, d), lambda i, j: (i, 0)),
                  pl.BlockSpec((d, tn), lambda i, j: (0, j))]
                 + [pl.BlockSpec((1, tn), lambda i, j: (0, j)) for _ in extra] + c_in,
        out_specs=[out_blk for _ in out_dtypes] + c_out,
        out_shape=[jax.ShapeDtypeStruct((m, width), dt) for dt in out_dtypes] + c_shapes,
        compiler_params=_cparams("parallel", "arbitrary"),
        name=name,
    )(h, w, *extra, *[c.src for c in riding])
    n_out = len(out_dtypes)
    return list(res[:n_out]) + _cast_results(casts, n_i * n_j, res[n_out:])


def _mid_rows(b, m):
    c, w = b.shape
    assert m in (1, 2, 4)
    if m == 1:
        odd = (lax.broadcasted_iota(jnp.int32, (c, w), 0) & 1) == 1
        return jnp.where(odd, pltpu.roll(b, 1, 0), b)
    b3 = b.reshape(c // 8, 8, w)
    if m == 4:
        return jnp.broadcast_to(b3[:, 3:4, :], b3.shape).reshape(c, w)
    low = lax.broadcasted_iota(jnp.int32, b3.shape, 1) < 4
    return jnp.where(low, jnp.broadcast_to(b3[:, 1:2, :], b3.shape),
                     jnp.broadcast_to(b3[:, 5:6, :], b3.shape)).reshape(c, w)


def _block_sums(tri, lf, lb):
    k = lf.shape[1]
    parts = []
    for a in (lf, lb):
        hi = a.astype(BF16)
        parts += [hi, (a - hi.astype(F32)).astype(BF16)]
    cs = jnp.dot(tri, jnp.concatenate(parts, axis=1), preferred_element_type=F32)
    return jnp.concatenate([cs[:, :k] + cs[:, k:2 * k], cs[:, 2 * k:3 * k] + cs[:, 3 * k:]], axis=1)


def _state_terms(v, kf, kb, bcat, lb):
    c, k = lb.shape
    tot = bcat[c - 1:c]
    ks = jnp.concatenate([kf * jnp.exp2(tot[:, :k] - bcat[:, :k]),
                          kb * jnp.exp2(bcat[:, k:] - lb)], axis=1)
    return pl.dot(v, ks.astype(BF16), trans_a=True), jnp.exp2(tot)


def _level_products(q, kf, kb, bcat, lb, par_ref):
    c, k = lb.shape
    xf, xb = bcat[:, :k], bcat[:, k:] - lb
    prods = []
    m, level = 1, 0
    while m < c:
        if m % 8 == 0:
            split = lambda a: a.reshape(c // (2 * m), 2, m, a.shape[-1])
            join = lambda even, odd: jnp.concatenate([even, odd], axis=1).reshape(c, k)
            xf4, xb4, kf4, kb4 = split(xf), split(xb), split(kf), split(kb)
            mid = split(bcat)[:, 0:1, m - 1:m, :]
            mf, mb = mid[..., :k], mid[..., k:]
            eq = join(mb - xb4[:, 0:1], xf4[:, 1:2] - mf)
            ek = join(mf - xf4[:, 0:1], xb4[:, 1:2] - mb)
            ksel = join(kf4[:, 0:1], kb4[:, 1:2])
        else:
            mid = _mid_rows(bcat, m)
            df = xf - mid[:, :k]
            db = mid[:, k:] - xb
            eq = jnp.minimum(df, db)
            ek = -jnp.maximum(df, db)
            ksel = jnp.where(par_ref[level] > 0, kb, kf)
        prods.append(pl.dot(q * jnp.exp2(eq), ksel * jnp.exp2(ek), trans_b=True))
        m *= 2
        level += 1
    prods.append(pl.dot(q, kf + kb, trans_b=True))
    return prods


def _assemble(prods):
    c = prods[0].shape[0]
    x = (lax.broadcasted_iota(jnp.int32, (c, c), 0) ^ lax.broadcasted_iota(jnp.int32, (c, c), 1))
    att = prods[-2]
    m = c // 2
    for a in reversed(prods[:-2]):
        att = jnp.where(x < m, a, att)
        m //= 2
    return jnp.where(x < 1, prods[-1], att).astype(BF16)


def _entry_factors(q, bcat, lb):
    c, k = lb.shape
    xb = bcat[:, k:] - lb
    return jnp.concatenate([q * jnp.exp2(bcat[:, :k]),
                            q * jnp.exp2(bcat[c - 1:c, k:] - xb)], axis=1).astype(BF16)


def _scan_kernel(nx, nc,
                 qx, vx, kfx, kbx, lfx, lbx, gx, qc, vc, kfc, kbc, lfc, lbc, gain_ref, tri_ref, par_ref,
                 y_ref, oi_ref, qi_ref, u_ref, d_ref, s_ref, b_ref):
    c = SCAN_BLOCK
    k = HEAD_DIM
    f32 = lambda ref, rows: ref[rows, :].astype(F32)

    for j in range(nc):
        rows = pl.ds(j * c, c)
        bcat = _block_sums(tri_ref[...], lfc[rows, :], lbc[rows, :])
        u_t, dec = _state_terms(vc[rows, :], f32(kfc, rows), f32(kbc, rows), bcat, lbc[rows, :])
        u_ref[j] = u_t
        d_ref[j] = dec

    group = math.gcd(nx, 4)
    n_groups = nx // group

    def sums(g, slot):
        for i in range(group):
            rows = pl.ds(pl.multiple_of((g * group + i) * c, c), c)
            b_ref[slot, pl.ds(i * c, c), :] = _block_sums(tri_ref[...], lfx[rows, :], lbx[rows, :])

    sums(0, 0)

    def do_group(p, slot):
        sums(jnp.minimum(p + 1, n_groups - 1), 1 - slot)
        rows = [pl.ds(pl.multiple_of((p * group + i) * c, c), c) for i in range(group)]
        bcats = [b_ref[slot, pl.ds(i * c, c), :] for i in range(group)]
        qkk = [(f32(qx, rw), f32(kfx, rw), f32(kbx, rw)) for rw in rows]
        prods = [_level_products(q, kf, kb, bcat, lbx[rw, :], par_ref)
                 for (q, kf, kb), rw, bcat in zip(qkk, rows, bcats)]
        for i, ((_, kf, kb), rw, bcat) in enumerate(zip(qkk, rows, bcats)):
            u_t, dec = _state_terms(vx[rw, :], kf, kb, bcat, lbx[rw, :])
            u_ref[nc + p * group + i] = u_t
            d_ref[nc + p * group + i] = dec
        outs = [jnp.dot(_assemble(pr), vx[rw, :], preferred_element_type=F32) for pr, rw in zip(prods, rows)]
        for (q, _, _), rw, bcat in zip(qkk, rows, bcats):
            qi_ref[rw, :] = _entry_factors(q, bcat, lbx[rw, :])
        for o, rw in zip(outs, rows):
            oi_ref[rw, :] = o

    def pair_body(pp, carry):
        do_group(2 * pp, 0)
        do_group(2 * pp + 1, 1)
        return carry

    lax.fori_loop(0, n_groups // 2, pair_body, 0)
    if n_groups % 2:
        do_group(n_groups - 1, 0)

    sf = jnp.zeros((k, k), F32)
    sb = jnp.zeros((k, k), F32)
    for j in range(nc):
        sf = d_ref[j][:, :k] * sf + u_ref[j][:, :k]
        jb = nc - 1 - j
        sb = d_ref[jb][:, k:] * sb + u_ref[jb][:, k:]

    def sweep_body(i, carry):
        sf, sb = carry
        s_ref[i, :, :k] = sf.astype(BF16)
        sf = d_ref[nc + i][:, :k] * sf + u_ref[nc + i][:, :k]
        j = nx - 1 - i
        s_ref[j, :, k:] = sb.astype(BF16)
        sb = d_ref[nc + j][:, k:] * sb + u_ref[nc + j][:, k:]
        return sf, sb

    lax.fori_loop(0, nx, sweep_body, (sf, sb))

    gain = gain_ref[...]

    def out_body(n, carry):
        rows = pl.ds(pl.multiple_of(n * c, c), c)
        o = oi_ref[rows, :] + pl.dot(qi_ref[rows, :], s_ref[n], trans_b=True)
        o = o * lax.rsqrt(jnp.mean(o * o, axis=-1, keepdims=True) + LN_EPS) * gain
        y_ref[rows, :] = (o * gx[rows, :].astype(F32)).astype(BF16)
        return carry

    lax.fori_loop(0, nx, out_body, 0, unroll=math.gcd(nx, 8))


def _hgrn2_scan(px, pc, gain, b, t, tc):
    qx, vx, kkx, lfx, gx = px
    qc, vc, kkc, lfc = pc
    k = HEAD_DIM
    heads = qx.shape[1] // k
    nx, nc = t // SCAN_BLOCK, tc // SCAN_BLOCK
    n_levels = SCAN_BLOCK.bit_length() - 1
    bits = (jnp.arange(SCAN_BLOCK)[None, :] >> jnp.arange(n_levels)[:, None]) & 1
    parity = jnp.broadcast_to(bits[:, :, None], (n_levels, SCAN_BLOCK, k)).astype(F32)
    fwd = lambda rows: pl.BlockSpec((rows, k), lambda i, h: (i, h))
    bwd = lambda rows: pl.BlockSpec((rows, k), lambda i, h: (i, h + heads))
    return pl.pallas_call(
        functools.partial(_scan_kernel, nx, nc),
        grid=(b, heads),
        in_specs=[fwd(t), fwd(t), fwd(t), bwd(t), fwd(t), bwd(t), fwd(t),
                  fwd(tc), fwd(tc), fwd(tc), bwd(tc), fwd(tc), bwd(tc),
                  pl.BlockSpec((1, k), lambda i, h: (0, 0)),
                  pl.BlockSpec((SCAN_BLOCK, SCAN_BLOCK), lambda i, h: (0, 0)),
                  pl.BlockSpec(parity.shape, lambda i, h: (0, 0, 0))],
        out_specs=fwd(t),
        out_shape=jax.ShapeDtypeStruct((b * t, heads * k), BF16),
        scratch_shapes=[pltpu.VMEM((t, k), F32),
                        pltpu.VMEM((t, 2 * k), BF16),
                        pltpu.VMEM((nx + nc, k, 2 * k), F32),
                        pltpu.VMEM((nx + nc, 1, 2 * k), F32),
                        pltpu.VMEM((nx, k, 2 * k), BF16),
                        pltpu.VMEM((2, math.gcd(nx, 4) * SCAN_BLOCK, 2 * k), F32)],
        compiler_params=_cparams("parallel", "parallel"),
        name="hgrn2_scan",
    )(qx, vx, kkx, kkx, lfx, lfx, gx, qc, vc, kkc, kkc, lfc, lfc, gain,
      jnp.tril(jnp.ones((SCAN_BLOCK, SCAN_BLOCK), BF16)), parity)


def _mix_kernel(n_groups, chunk, u_ref, v_ref, g_ref, b_ref, ws_ref, bias_ref, y_ref, vn_ref):
    vn_ref[...] = (_ln_rows(v_ref[...].astype(F32)) * g_ref[...] + b_ref[...]).astype(BF16)
    gc = v_ref.shape[1] // n_groups
    for cidx in range(v_ref.shape[0] // chunk):
        rows = pl.ds(cidx * chunk, chunk)
        for g in range(n_groups):
            cols = pl.ds(g * gc, gc)
            mixed = jnp.dot(ws_ref[g], vn_ref[rows, cols], preferred_element_type=F32)
            y_ref[rows, cols] = (u_ref[rows, cols].astype(F32) * (mixed + bias_ref[:, cols])).astype(BF16)


def _chunk_mix(uv, v_g, v_b, w_s, bias, rows_per_step):
    m = uv.shape[0]
    bw = uv.shape[1] // 2
    groups, chunk, _ = w_s.shape
    tm = rows_per_step
    return pl.pallas_call(
        functools.partial(_mix_kernel, groups, chunk),
        grid=(m // tm,),
        in_specs=[pl.BlockSpec((tm, bw), lambda i: (i, 0)),
                  pl.BlockSpec((tm, bw), lambda i: (i, 1)),
                  pl.BlockSpec((1, bw), lambda i: (0, 0)),
                  pl.BlockSpec((1, bw), lambda i: (0, 0)),
                  pl.BlockSpec((groups, chunk, chunk), lambda i: (0, 0, 0)),
                  pl.BlockSpec((chunk, bw), lambda i: (0, 0))],
        out_specs=pl.BlockSpec((tm, bw), lambda i: (i, 0)),
        out_shape=jax.ShapeDtypeStruct((m, bw), BF16),
        scratch_shapes=[pltpu.VMEM((tm, bw), BF16)],
        compiler_params=_cparams("parallel"),
        name="chunk_mix",
    )(uv, uv, v_g, v_b, w_s, bias)


def _merge_kernel(n_cast, ya_ref, yb_ref, wa_ref, wb_ref, ga_ref, gb_ref, *refs):
    cast_src, o_ref, cast_dst = refs[:n_cast], refs[n_cast], refs[n_cast + 1:]
    a = jnp.dot(ya_ref[...], wa_ref[...], preferred_element_type=F32)
    bq = jnp.dot(yb_ref[...], wb_ref[...], preferred_element_type=F32)
    o_ref[...] = (ga_ref[...].astype(F32) * a + gb_ref[...].astype(F32) * bq).astype(BF16)
    _do_casts(cast_src, cast_dst)


def _merge(ya, yb, wa, wb, gates, casts=()):
    m, ka = ya.shape
    kb = yb.shape[1]
    d = wa.shape[1]
    tm, tn = min(MXU_TILE, m), min(MXU_TILE, d)
    nd = d // tn
    steps = (m // tm) * nd
    riding = _split_casts(casts, steps)
    c_in, c_out, c_shapes = _cast_specs(riding, steps, lambda i, j: i * nd + j)
    res = pl.pallas_call(
        functools.partial(_merge_kernel, len(riding)),
        grid=(m // tm, nd),
        in_specs=[pl.BlockSpec((tm, ka), lambda i, j: (i, 0)),
                  pl.BlockSpec((tm, kb), lambda i, j: (i, 0)),
                  pl.BlockSpec((ka, tn), lambda i, j: (0, j)),
                  pl.BlockSpec((kb, tn), lambda i, j: (0, j)),
                  pl.BlockSpec((tm, tn), lambda i, j: (i, j)),
                  pl.BlockSpec((tm, tn), lambda i, j: (i, j + nd))] + c_in,
        out_specs=[pl.BlockSpec((tm, tn), lambda i, j: (i, j))] + c_out,
        out_shape=[jax.ShapeDtypeStruct((m, d), BF16)] + c_shapes,
        compiler_params=_cparams("parallel", "arbitrary"),
        name="branch_merge",
    )(ya, yb, wa, wb, gates, gates, *[c.src for c in riding])
    return [res[0]] + _cast_results(casts, steps, res[1:])


def _resid_pos_kernel(alpha, n_row_code, x_ref, w_ref, r_ref, rt_ref, ct_ref, gate_ref, o_ref):
    y = jnp.dot(x_ref[...], w_ref[...], preferred_element_type=F32)
    is_row = (pl.program_id(1) < n_row_code).astype(F32)
    xp = r_ref[...] + is_row * rt_ref[...] + (1.0 - is_row) * ct_ref[...]
    o_ref[...] = alpha * xp.reshape(o_ref.shape) + gate_ref[...] * y


def _out_proj_resid(xin, w, x, row_tab, col_tab, mod3, gate_idx, alpha):
    m, kdim = xin.shape
    b, t, d = x.shape
    half = d // 2
    tm, tn = min(MXU_TILE, m, t), min(MXU_TILE, half)
    assert tm % GRID_W == 0 and half % tn == 0
    per_batch = t // tm
    nd, nh = d // tn, half // tn
    r = tm // GRID_W
    x3 = x.reshape(m // GRID_W, GRID_W, d)
    return pl.pallas_call(
        functools.partial(_resid_pos_kernel, alpha, nh),
        grid=(m // tm, nd),
        in_specs=[pl.BlockSpec((tm, kdim), lambda i, j: (i, 0)),
                  pl.BlockSpec((kdim, tn), lambda i, j: (0, j)),
                  pl.BlockSpec((r, GRID_W, tn), lambda i, j: (i, 0, j)),
                  pl.BlockSpec((r, 1, tn), lambda i, j: (i % per_batch, 0, jnp.minimum(j, nh - 1))),
                  pl.BlockSpec((1, GRID_W, tn), lambda i, j: (0, 0, jnp.maximum(j - nh, 0))),
                  pl.BlockSpec((None, 1, tn), lambda i, j: (i // per_batch, 0, gate_idx * nd + j))],
        out_specs=pl.BlockSpec((tm, tn), lambda i, j: (i, j)),
        out_shape=jax.ShapeDtypeStruct((m, d), F32),
        compiler_params=_cparams("parallel", "arbitrary", vmem=VMEM_LIMIT_WIDE),
        name="out_proj_resid",
    )(xin, w, x3, row_tab, col_tab, mod3)


def _ln_ffn_up_kernel(n_cast, p_ref, g_ref, b_ref, sh_ref, sc_ref, w_ref, *refs):
    cast_src, refs = refs[:n_cast], refs[n_cast:]
    x1_ref, act_ref = refs[0], refs[1]
    cast_dst, h_ref = refs[2:2 + n_cast], refs[2 + n_cast]
    i, j = pl.program_id(0), pl.program_id(1)
    rows = p_ref.shape[0]

    def prepare():
        x = _ln_rows(p_ref[...]) * g_ref[...] + b_ref[...]
        x1_ref[...] = x
        h = _ln_rows(x) * (1.0 + sc_ref[...]) + sh_ref[...]
        h_ref[i % 2, pl.ds(pl.multiple_of(j * rows, rows), rows), :] = h.astype(BF16)
        _do_casts(cast_src, cast_dst)

    @pl.when(i == 0)
    def _():
        prepare()

    @pl.when(i > 0)
    def _():
        y = jnp.dot(h_ref[(i - 1) % 2], w_ref[...], preferred_element_type=F32)
        act_ref[...] = jnp.square(jnp.maximum(y, 0.0)).astype(BF16)
        prepare()


def _ln_ffn_up(pre, g, bvec, mod3, shift_idx, scale_idx, t, w, casts=()):
    m, d = pre.shape
    n = w.shape[1]
    tm, tn = min(MXU_TILE, m, t), min(MXU_TILE, n)
    n_i, n_j = m // tm, n // tn
    rows = tm // n_j
    assert rows * n_j == tm and rows % BF16_SUBLANES == 0
    per_batch = t // tm
    tile = lambda i: jnp.minimum(i, n_i - 1)
    prev = lambda i: jnp.maximum(i - 1, 0)
    steps = n_i * n_j
    slab = pl.BlockSpec((rows, d), lambda i, j: (jnp.minimum(i * n_j + j, steps - 1), 0))
    vec = pl.BlockSpec((1, d), lambda i, j: (0, 0))
    riding = _split_casts(casts, steps)
    c_in, c_out, c_shapes = _cast_specs(riding, steps, lambda i, j: jnp.where(i > 0, (i - 1) * n_j + j, 0))
    res = pl.pallas_call(
        functools.partial(_ln_ffn_up_kernel, len(riding)),
        grid=(n_i + 1, n_j),
        in_specs=[slab, vec, vec,
                  pl.BlockSpec((None, 1, d), lambda i, j: (tile(i) // per_batch, 0, shift_idx)),
                  pl.BlockSpec((None, 1, d), lambda i, j: (tile(i) // per_batch, 0, scale_idx)),
                  pl.BlockSpec((d, tn), lambda i, j: (0, jnp.where(i > 0, j, 0)))] + c_in,
        out_specs=[slab, pl.BlockSpec((tm, tn), lambda i, j: (prev(i), jnp.where(i > 0, j, 0)))] + c_out,
        out_shape=[jax.ShapeDtypeStruct((m, d), F32), jax.ShapeDtypeStruct((m, n), BF16)] + c_shapes,
        scratch_shapes=[pltpu.VMEM((2, tm, d), BF16)],
        compiler_params=_cparams("arbitrary", "arbitrary"),
        name="ln_ffn_up",
    )(pre, g, bvec, mod3, mod3, w, *[c.src for c in riding])
    return [res[0], res[1]] + _cast_results(casts, steps, res[2:])


def _ln_kernel(p_ref, g_ref, b_ref, o_ref):
    o_ref[...] = _ln_rows(p_ref[...]) * g_ref[...] + b_ref[...]


def _ln(pre, g, bvec, tm):
    m, d = pre.shape
    row = pl.BlockSpec((tm, d), lambda i: (i, 0))
    vec = pl.BlockSpec((1, d), lambda i: (0, 0))
    return pl.pallas_call(
        _ln_kernel,
        grid=(m // tm,),
        in_specs=[row, vec, vec],
        out_specs=row,
        out_shape=jax.ShapeDtypeStruct((m, d), F32),
        compiler_params=_cparams("parallel"),
        name="ln_out",
    )(pre, g, bvec)


def _ffn2_kernel(alpha, a_ref, w_ref, r_ref, gate_ref, o_ref):
    kk = pl.program_id(2)

    @pl.when(kk == 0)
    def _():
        o_ref[...] = jnp.dot(a_ref[...], w_ref[...], preferred_element_type=F32)

    @pl.when(kk > 0)
    def _():
        o_ref[...] += jnp.dot(a_ref[...], w_ref[...], preferred_element_type=F32)

    @pl.when(kk == pl.num_programs(2) - 1)
    def _():
        o_ref[...] = alpha * r_ref[...] + gate_ref[...] * o_ref[...]


def _ffn2_resid(a, w, resid, mod3, gate_idx, alpha, t):
    m, f = a.shape
    d = w.shape[1]
    tm, tn, tk = min(MXU_TILE, m, t), min(MXU_TILE, d), min(4 * MXU_TILE, f)
    per_batch = t // tm
    nd = d // tn
    return pl.pallas_call(
        functools.partial(_ffn2_kernel, alpha),
        grid=(m // tm, nd, f // tk),
        in_specs=[pl.BlockSpec((tm, tk), lambda i, j, kk: (i, kk)),
                  pl.BlockSpec((tk, tn), lambda i, j, kk: (kk, j)),
                  pl.BlockSpec((tm, tn), lambda i, j, kk: (i, j)),
                  pl.BlockSpec((None, 1, tn), lambda i, j, kk: (i // per_batch, 0, gate_idx * nd + j))],
        out_specs=pl.BlockSpec((tm, tn), lambda i, j, kk: (i, j)),
        out_shape=jax.ShapeDtypeStruct((m, d), F32),
        compiler_params=_cparams("parallel", "parallel", "arbitrary"),
        name="ffn_down_resid",
    )(a, w, resid, mod3)


def _pos_tables(rows, cols, dim):
    quarter = dim // 4
    omega = 1.0 / (POS_BASE ** (jnp.arange(quarter, dtype=F32) / quarter))

    def emb(p):
        a = p[:, None] * omega[None, :]
        return jnp.concatenate([jnp.sin(a), jnp.cos(a)], axis=-1)

    return (emb(jnp.arange(rows, dtype=F32))[:, None, :], emb(jnp.arange(cols, dtype=F32))[None, :, :])


def kernel(x, c, ctx, c_ctx, w_ada, b_ada, w_in, lb_logits, a_norm_g, w_proj_a, v_norm_g, v_norm_b,
           w_s, b_s, w_proj_b, w_out, ln1_g, ln1_b, w_ff1, w_ff2, ln2_g, ln2_b):
    depth = w_ada.shape[0]
    assert depth == 1, "only the single-layer block is implemented"
    b, t, d = x.shape
    tc = ctx.shape[1]
    a_fdim = lb_logits.shape[-1]
    a_width = w_proj_a.shape[1]
    b_width = w_proj_b.shape[1]
    assert a_fdim == a_width and a_norm_g.shape[-1] == HEAD_DIM
    assert t % GRID_W == 0 and t % SCAN_BLOCK == 0 and tc % SCAN_BLOCK == 0
    assert w_s.shape[2] == SCAN_BLOCK or t % w_s.shape[2] == 0
    alpha = (2.0 * depth) ** 0.25
    m = b * t

    sizes = (a_fdim, a_width, a_fdim, a_fdim, a_width, b_width, b_width, d, d)
    offs = [0]
    for s in sizes:
        offs.append(offs[-1] + s)
    assert offs[-1] == w_in.shape[-1]

    w_in0 = w_in[0]
    seg = lambda first, last: _Cast(w_in0, offs[first], offs[last + 1] - offs[first])
    wq_b = w_in0[:, :offs[1]].astype(BF16)
    ws_b = w_s[0].astype(BF16)
    lb_all = jnp.cumsum(jax.nn.softmax(lb_logits.astype(F32), axis=1), axis=1)
    lb_row = jnp.concatenate([lb_all[0, 0], lb_all[1, 0]])[None, :]
    mix_bias = jnp.repeat(b_s[0].T, b_width // w_s.shape[1], axis=1)
    row_tab, col_tab = _pos_tables(t // GRID_W, GRID_W, d)

    n_rows = -(-(b + 1) // 8) * 8
    cc = jnp.concatenate([c, c_ctx[None, :], jnp.zeros((n_rows - b - 1, d), F32)], axis=0)
    mod = _modulation(cc, w_ada[0], b_ada[0][None, :])
    mod3 = mod.reshape(n_rows, 1, N_MOD * d)

    h = _adaln_pos(x, row_tab, col_tab, mod3, rows_per_step=min(4, t // GRID_W))
    hc = _adaln_ctx(ctx.reshape(b * tc, d), mod3, b, tm=min(256, b * tc))

    scale = HEAD_DIM ** -0.5
    ep_q = functools.partial(_ep_silu, scale)
    ep_g = functools.partial(_ep_silu, 1.0)

    qx, wi_b = _in_proj(h, wq_b, ep_q, [BF16], casts=(seg(1, 1),), name="in_proj_q")
    vx, wf_b = _in_proj(h, wi_b, _ep_cast, [BF16], casts=(seg(2, 3),), name="in_proj_i")
    kkx, lfx, wg_b = _in_proj(h, wf_b, _ep_decay, [BF16, F32], extra=(lb_row,), casts=(seg(4, 4),),
                              name="in_proj_f")
    gx, wuv_b = _in_proj(h, wg_b, ep_g, [BF16], casts=(seg(5, 6),), name="in_proj_g")
    uv, wgate_b = _in_proj(h, wuv_b, _ep_cast, [BF16], casts=(seg(7, 8),), name="in_proj_uv")
    gates, wa_b, wb_b, w1_b = _in_proj(
        h, wgate_b, _ep_sigmoid, [BF16],
        casts=(_Cast(w_proj_a[0], 0, d), _Cast(w_proj_b[0], 0, d), _Cast(w_ff1[0], 0, w_ff1.shape[-1])),
        name="in_proj_gates")

    (qc,) = _in_proj(hc, wq_b, ep_q, [BF16], name="in_proj_q_ctx")
    (vc,) = _in_proj(hc, wi_b, _ep_cast, [BF16], name="in_proj_i_ctx")
    kkc, lfc = _in_proj(hc, wf_b, _ep_decay, [BF16, F32], extra=(lb_row,), name="in_proj_f_ctx")

    ya = _hgrn2_scan((qx, vx, kkx, lfx, gx), (qc, vc, kkc, lfc), a_norm_g[0][None, :], b, t, tc)
    yb = _chunk_mix(uv, v_norm_g[0][None, :], v_norm_b[0][None, :], ws_b, mix_bias,
                    rows_per_step=min(2 * w_s.shape[2], t))

    merged, wo_b = _merge(ya, yb, wa_b, wb_b, gates, casts=(_Cast(w_out[0], 0, d),))
    pre1 = _out_proj_resid(merged, wo_b, x, row_tab, col_tab, mod3, 2, alpha)
    x1, act, w2_b = _ln_ffn_up(pre1, ln1_g[0][None, :], ln1_b[0][None, :], mod3, 3, 4, t, w1_b,
                               casts=(_Cast(w_ff2[0], 0, d),))
    pre2 = _ffn2_resid(act, w2_b, x1, mod3, 5, alpha, t)
    out = _ln(pre2, ln2_g[0][None, :], ln2_b[0][None, :], tm=min(256, m))
    return out.reshape(b, t, d)
```

```python
import functools
import math
from typing import NamedTuple

import jax
import jax.numpy as jnp
from jax import lax
from jax.experimental import pallas as pl
from jax.experimental.pallas import tpu as pltpu

F32 = jnp.float32
BF16 = jnp.bfloat16

GRID_W = 64
LN_EPS = 1e-6
POS_BASE = 10000.0
N_MOD = 6
HEAD_DIM = 128
SCAN_BLOCK = 128
MXU_TILE = 1024
CAST_COLS = 4096
BF16_SUBLANES = 16
VMEM_LIMIT = 56 * 1024 * 1024
VMEM_LIMIT_WIDE = 60 * 1024 * 1024


def _cparams(*sem, vmem=VMEM_LIMIT):
    return pltpu.CompilerParams(dimension_semantics=sem, vmem_limit_bytes=vmem)


def _sigmoid(x):
    return 0.5 * jnp.tanh(0.5 * x) + 0.5


def _ln_rows(x):
    mu = jnp.mean(x, axis=-1, keepdims=True)
    xc = x - mu
    return xc * lax.rsqrt(jnp.mean(xc * xc, axis=-1, keepdims=True) + LN_EPS)


def _mod_kernel(c_ref, w_ref, b_ref, o_ref):
    a = c_ref[...]
    a = a * _sigmoid(a)
    o_ref[...] = jnp.dot(a.astype(BF16), w_ref[...].astype(BF16),
                         preferred_element_type=F32) + b_ref[...]


def _modulation(cc, w_ada, b_ada):
    rows, d = cc.shape
    n = w_ada.shape[1]
    tn = min(512, n)
    return pl.pallas_call(
        _mod_kernel,
        grid=(n // tn,),
        in_specs=[pl.BlockSpec((rows, d), lambda j: (0, 0)),
                  pl.BlockSpec((d, tn), lambda j: (0, j)),
                  pl.BlockSpec((1, tn), lambda j: (0, j))],
        out_specs=pl.BlockSpec((rows, tn), lambda j: (0, j)),
        out_shape=jax.ShapeDtypeStruct((rows, n), F32),
        compiler_params=_cparams("arbitrary"),
        name="modulation",
    )(cc, w_ada, b_ada)


def _adaln_pos_kernel(x_ref, rt_ref, ct_ref, sh_ref, sc_ref, h_ref):
    half = rt_ref.shape[-1]
    d = 2 * half
    xa = x_ref[:, :, :half] + rt_ref[...]
    xb = x_ref[:, :, half:] + ct_ref[...]
    mu = (jnp.sum(xa, axis=-1, keepdims=True) + jnp.sum(xb, axis=-1, keepdims=True)) * (1.0 / d)
    xa = xa - mu
    xb = xb - mu
    var = (jnp.sum(xa * xa, axis=-1, keepdims=True) + jnp.sum(xb * xb, axis=-1, keepdims=True)) * (1.0 / d)
    inv = lax.rsqrt(var + LN_EPS)
    sh = sh_ref[...]
    sc = 1.0 + sc_ref[...]
    h_ref[:, :, :half] = (xa * inv * sc[:, :half] + sh[:, :half]).astype(BF16)
    h_ref[:, :, half:] = (xb * inv * sc[:, half:] + sh[:, half:]).astype(BF16)


def _adaln_pos(x, row_tab, col_tab, mod3, rows_per_step):
    b, t, d = x.shape
    gr = t // GRID_W
    r = rows_per_step
    x4 = x.reshape(b, gr, GRID_W, d)
    blk = pl.BlockSpec((None, r, GRID_W, d), lambda i, j: (i, j, 0, 0))
    h = pl.pallas_call(
        _adaln_pos_kernel,
        grid=(b, gr // r),
        in_specs=[blk,
                  pl.BlockSpec((r, 1, d // 2), lambda i, j: (j, 0, 0)),
                  pl.BlockSpec((1, GRID_W, d // 2), lambda i, j: (0, 0, 0)),
                  pl.BlockSpec((None, 1, d), lambda i, j: (i, 0, 0)),
                  pl.BlockSpec((None, 1, d), lambda i, j: (i, 0, 1))],
        out_specs=blk,
        out_shape=jax.ShapeDtypeStruct(x4.shape, BF16),
        compiler_params=_cparams("parallel", "parallel"),
        name="adaln_pos",
    )(x4, row_tab, col_tab, mod3, mod3)
    return h.reshape(b * t, d)


def _adaln_rows_kernel(x_ref, sh_ref, sc_ref, h_ref):
    y = _ln_rows(x_ref[...])
    h_ref[...] = (y * (1.0 + sc_ref[...]) + sh_ref[...]).astype(BF16)


def _adaln_ctx(xc, mod3, ctx_row, tm):
    m, d = xc.shape
    return pl.pallas_call(
        _adaln_rows_kernel,
        grid=(m // tm,),
        in_specs=[pl.BlockSpec((tm, d), lambda i: (i, 0)),
                  pl.BlockSpec((None, 1, d), lambda i: (ctx_row, 0, 0)),
                  pl.BlockSpec((None, 1, d), lambda i: (ctx_row, 0, 1))],
        out_specs=pl.BlockSpec((tm, d), lambda i: (i, 0)),
        out_shape=jax.ShapeDtypeStruct((m, d), BF16),
        compiler_params=_cparams("parallel"),
        name="adaln_ctx",
    )(xc, mod3, mod3)


def _ep_silu(scale, acc, o_ref):
    hz = 0.5 * acc
    o_ref[...] = ((hz if scale == 1.0 else hz * scale) * (1.0 + jnp.tanh(hz))).astype(o_ref.dtype)


def _ep_cast(acc, o_ref):
    o_ref[...] = acc.astype(o_ref.dtype)


def _ep_sigmoid(acc, o_ref):
    o_ref[...] = _sigmoid(acc).astype(o_ref.dtype)


def _ep_relu2(acc, o_ref):
    o_ref[...] = jnp.square(jnp.maximum(acc, 0.0)).astype(o_ref.dtype)


def _ep_decay(acc, lb_ref, k_ref, lf_ref):
    lb = lb_ref[...]
    c = 0.5 - 0.5 * lb
    ct = c * jnp.tanh(0.5 * acc)
    k_ref[...] = (c - ct).astype(k_ref.dtype)
    lf_ref[...] = jnp.log2((0.5 + 0.5 * lb) + ct)


class _Cast(NamedTuple):
    src: jax.Array
    col0: int
    width: int


def _cast_blocks(cast, steps):
    cw = math.gcd(cast.col0, cast.width, CAST_COLS)
    ncb = cast.width // cw
    return (cast.src.shape[0] * ncb) // steps, cw, ncb


def _cast_rides(cast, steps):
    rb, _, ncb = _cast_blocks(cast, steps)
    rows = cast.src.shape[0]
    return rb * steps == rows * ncb and rb % BF16_SUBLANES == 0 and rows % rb == 0


def _split_casts(casts, steps):
    return [c for c in casts if _cast_rides(c, steps)]


def _cast_results(casts, steps, ridden):
    ridden = list(ridden)
    return [ridden.pop(0) if _cast_rides(c, steps) else c.src[:, c.col0:c.col0 + c.width].astype(BF16)
            for c in casts]


def _cast_specs(casts, steps, step_of):
    in_specs, out_specs, out_shapes = [], [], []
    for cast in casts:
        src, col0, width = cast
        rows = src.shape[0]
        rb, cw, ncb = _cast_blocks(cast, steps)
        coff = col0 // cw
        in_specs.append(pl.BlockSpec(
            (rb, cw), lambda i, j, ncb=ncb, coff=coff: (step_of(i, j) // ncb, coff + step_of(i, j) % ncb)))
        out_specs.append(pl.BlockSpec(
            (rb, cw), lambda i, j, ncb=ncb: (step_of(i, j) // ncb, step_of(i, j) % ncb)))
        out_shapes.append(jax.ShapeDtypeStruct((rows, width), BF16))
    return in_specs, out_specs, out_shapes


def _do_casts(src_refs, dst_refs):
    for s_ref, d_ref in zip(src_refs, dst_refs, strict=True):
        d_ref[...] = s_ref[...].astype(BF16)


def _proj_kernel(epilogue, n_extra, n_cast, x_ref, w_ref, *refs):
    extras, refs = refs[:n_extra], refs[n_extra:]
    cast_src, refs = refs[:n_cast], refs[n_cast:]
    outs, cast_dst = refs[:len(refs) - n_cast], refs[len(refs) - n_cast:]
    acc = jnp.dot(x_ref[...], w_ref[...], preferred_element_type=F32)
    epilogue(acc, *extras, *outs)
    _do_casts(cast_src, cast_dst)


def _in_proj(h, w, epilogue, out_dtypes, extra=(), casts=(), name="in_proj"):
    m, d = h.shape
    width = w.shape[1]
    tm = min(MXU_TILE, m)
    tn = min(MXU_TILE, width)
    n_i, n_j = m // tm, width // tn
    out_blk = pl.BlockSpec((tm, tn), lambda i, j: (i, j))
    riding = _split_casts(casts, n_i * n_j)
    c_in, c_out, c_shapes = _cast_specs(riding, n_i * n_j, lambda i, j: i * n_j + j)
    res = pl.pallas_call(
        functools.partial(_proj_kernel, epilogue, len(extra), len(riding)),
        grid=(n_i, n_j),
        in_specs=[pl.BlockSpec((tm, d), lambda i, j: (i, 0)),
                  pl.BlockSpec((d, tn), lambda i, j: (0, j))]
                 + [pl.BlockSpec((1, tn), lambda i, j: (0, j)) for _ in extra] + c_in,
        out_specs=[out_blk for _ in out_dtypes] + c_out,
        out_shape=[jax.ShapeDtypeStruct((m, width), dt) for dt in out_dtypes] + c_shapes,
        compiler_params=_cparams("parallel", "arbitrary"),
        name=name,
    )(h, w, *extra, *[c.src for c in riding])
    n_out = len(out_dtypes)
    return list(res[:n_out]) + _cast_results(casts, n_i * n_j, res[n_out:])


def _mid_rows(b, m):
    c, w = b.shape
    assert m in (1, 2, 4)
    if m == 1:
        odd = (lax.broadcasted_iota(jnp.int32, (c, w), 0) & 1) == 1
        return jnp.where(odd, pltpu.roll(b, 1, 0), b)
    b3 = b.reshape(c // 8, 8, w)
    if m == 4:
        return jnp.broadcast_to(b3[:, 3:4, :], b3.shape).reshape(c, w)
    low = lax.broadcasted_iota(jnp.int32, b3.shape, 1) < 4
    return jnp.where(low, jnp.broadcast_to(b3[:, 1:2, :], b3.shape),
                     jnp.broadcast_to(b3[:, 5:6, :], b3.shape)).reshape(c, w)


def _block_sums(tri, lf, lb):
    k = lf.shape[1]
    parts = []
    for a in (lf, lb):
        hi = a.astype(BF16)
        parts += [hi, (a - hi.astype(F32)).astype(BF16)]
    cs = jnp.dot(tri, jnp.concatenate(parts, axis=1), preferred_element_type=F32)
    return jnp.concatenate([cs[:, :k] + cs[:, k:2 * k], cs[:, 2 * k:3 * k] + cs[:, 3 * k:]], axis=1)


def _state_terms(v, kf, kb, bcat, lb):
    c, k = lb.shape
    tot = bcat[c - 1:c]
    ks = jnp.concatenate([kf * jnp.exp2(tot[:, :k] - bcat[:, :k]),
                          kb * jnp.exp2(bcat[:, k:] - lb)], axis=1)
    return pl.dot(v, ks.astype(BF16), trans_a=True), jnp.exp2(tot)


def _level_products(q, kf, kb, bcat, lb, par_ref):
    c, k = lb.shape
    xf, xb = bcat[:, :k], bcat[:, k:] - lb
    prods = []
    m, level = 1, 0
    while m < c:
        if m % 8 == 0:
            split = lambda a: a.reshape(c // (2 * m), 2, m, a.shape[-1])
            join = lambda even, odd: jnp.concatenate([even, odd], axis=1).reshape(c, k)
            xf4, xb4, kf4, kb4 = split(xf), split(xb), split(kf), split(kb)
            mid = split(bcat)[:, 0:1, m - 1:m, :]
            mf, mb = mid[..., :k], mid[..., k:]
            eq = join(mb - xb4[:, 0:1], xf4[:, 1:2] - mf)
            ek = join(mf - xf4[:, 0:1], xb4[:, 1:2] - mb)
            ksel = join(kf4[:, 0:1], kb4[:, 1:2])
        else:
            mid = _mid_rows(bcat, m)
            df = xf - mid[:, :k]
            db = mid[:, k:] - xb
            eq = jnp.minimum(df, db)
            ek = -jnp.maximum(df, db)
            ksel = jnp.where(par_ref[level] > 0, kb, kf)
        prods.append(pl.dot(q * jnp.exp2(eq), ksel * jnp.exp2(ek), trans_b=True))
        m *= 2
        level += 1
    prods.append(pl.dot(q, kf + kb, trans_b=True))
    return prods


def _assemble(prods):
    c = prods[0].shape[0]
    x = (lax.broadcasted_iota(jnp.int32, (c, c), 0) ^ lax.broadcasted_iota(jnp.int32, (c, c), 1))
    att = prods[-2]
    m = c // 2
    for a in reversed(prods[:-2]):
        att = jnp.where(x < m, a, att)
        m //= 2
    return jnp.where(x < 1, prods[-1], att).astype(BF16)


def _entry_factors(q, bcat, lb):
    c, k = lb.shape
    xb = bcat[:, k:] - lb
    return jnp.concatenate([q * jnp.exp2(bcat[:, :k]),
                            q * jnp.exp2(bcat[c - 1:c, k:] - xb)], axis=1).astype(BF16)


def _scan_kernel(nx, nc,
                 qx, vx, kfx, kbx, lfx, lbx, gx, qc, vc, kfc, kbc, lfc, lbc, gain_ref, tri_ref, par_ref,
                 y_ref, oi_ref, qi_ref, u_ref, d_ref, s_ref, b_ref):
    c = SCAN_BLOCK
    k = HEAD_DIM
    f32 = lambda ref, rows: ref[rows, :].astype(F32)

    ctx_rows = [pl.ds(j * c, c) for j in range(nc)]
    ctx_sums = [_block_sums(tri_ref[...], lfc[rows, :], lbc[rows, :]) for rows in ctx_rows]

    group = math.gcd(nx, 4)
    n_groups = nx // group

    def sums(g, slot):
        for i in range(group):
            rows = pl.ds(pl.multiple_of((g * group + i) * c, c), c)
            b_ref[slot, pl.ds(i * c, c), :] = _block_sums(tri_ref[...], lfx[rows, :], lbx[rows, :])

    sums(0, 0)
    for j, (rows, bcat) in enumerate(zip(ctx_rows, ctx_sums)):
        u_t, dec = _state_terms(vc[rows, :], f32(kfc, rows), f32(kbc, rows), bcat, lbc[rows, :])
        u_ref[j] = u_t
        d_ref[j] = dec

    def do_group(p, slot):
        sums(jnp.minimum(p + 1, n_groups - 1), 1 - slot)
        rows = [pl.ds(pl.multiple_of((p * group + i) * c, c), c) for i in range(group)]
        bcats = [b_ref[slot, pl.ds(i * c, c), :] for i in range(group)]
        qkk = [(f32(qx, rw), f32(kfx, rw), f32(kbx, rw)) for rw in rows]
        prods = [_level_products(q, kf, kb, bcat, lbx[rw, :], par_ref)
                 for (q, kf, kb), rw, bcat in zip(qkk, rows, bcats)]
        for i, ((_, kf, kb), rw, bcat) in enumerate(zip(qkk, rows, bcats)):
            u_t, dec = _state_terms(vx[rw, :], kf, kb, bcat, lbx[rw, :])
            u_ref[nc + p * group + i] = u_t
            d_ref[nc + p * group + i] = dec
        outs = [jnp.dot(_assemble(pr), vx[rw, :], preferred_element_type=F32) for pr, rw in zip(prods, rows)]
        for (q, _, _), rw, bcat in zip(qkk, rows, bcats):
            qi_ref[rw, :] = _entry_factors(q, bcat, lbx[rw, :])
        for o, rw in zip(outs, rows):
            oi_ref[rw, :] = o

    def pair_body(pp, carry):
        do_group(2 * pp, 0)
        do_group(2 * pp + 1, 1)
        return carry

    lax.fori_loop(0, n_groups // 2, pair_body, 0)
    if n_groups % 2:
        do_group(n_groups - 1, 0)

    sf = jnp.zeros((k, k), F32)
    sb = jnp.zeros((k, k), F32)
    for j in range(nc):
        sf = d_ref[j][:, :k] * sf + u_ref[j][:, :k]
        jb = nc - 1 - j
        sb = d_ref[jb][:, k:] * sb + u_ref[jb][:, k:]

    def sweep_body(i, carry):
        sf, sb = carry
        s_ref[i, :, :k] = sf.astype(BF16)
        sf = d_ref[nc + i][:, :k] * sf + u_ref[nc + i][:, :k]
        j = nx - 1 - i
        s_ref[j, :, k:] = sb.astype(BF16)
        sb = d_ref[nc + j][:, k:] * sb + u_ref[nc + j][:, k:]
        return sf, sb

    lax.fori_loop(0, nx, sweep_body, (sf, sb))

    gain = gain_ref[...]

    def out_body(n, carry):
        rows = pl.ds(pl.multiple_of(n * c, c), c)
        o = oi_ref[rows, :] + pl.dot(qi_ref[rows, :], s_ref[n], trans_b=True)
        o = o * lax.rsqrt(jnp.mean(o * o, axis=-1, keepdims=True) + LN_EPS) * gain
        y_ref[rows, :] = (o * gx[rows, :].astype(F32)).astype(BF16)
        return carry

    lax.fori_loop(0, nx, out_body, 0, unroll=math.gcd(nx, 16))


def _hgrn2_scan(px, pc, gain, b, t, tc):
    qx, vx, kkx, lfx, gx = px
    qc, vc, kkc, lfc = pc
    k = HEAD_DIM
    heads = qx.shape[1] // k
    nx, nc = t // SCAN_BLOCK, tc // SCAN_BLOCK
    n_levels = SCAN_BLOCK.bit_length() - 1
    bits = (jnp.arange(SCAN_BLOCK)[None, :] >> jnp.arange(n_levels)[:, None]) & 1
    parity = jnp.broadcast_to(bits[:, :, None], (n_levels, SCAN_BLOCK, k)).astype(F32)
    fwd = lambda rows: pl.BlockSpec((rows, k), lambda i, h: (i, h))
    bwd = lambda rows: pl.BlockSpec((rows, k), lambda i, h: (i, h + heads))
    return pl.pallas_call(
        functools.partial(_scan_kernel, nx, nc),
        grid=(b, heads),
        in_specs=[fwd(t), fwd(t), fwd(t), bwd(t), fwd(t), bwd(t), fwd(t),
                  fwd(tc), fwd(tc), fwd(tc), bwd(tc), fwd(tc), bwd(tc),
                  pl.BlockSpec((1, k), lambda i, h: (0, 0)),
                  pl.BlockSpec((SCAN_BLOCK, SCAN_BLOCK), lambda i, h: (0, 0)),
                  pl.BlockSpec(parity.shape, lambda i, h: (0, 0, 0))],
        out_specs=fwd(t),
        out_shape=jax.ShapeDtypeStruct((b * t, heads * k), BF16),
        scratch_shapes=[pltpu.VMEM((t, k), F32),
                        pltpu.VMEM((t, 2 * k), BF16),
                        pltpu.VMEM((nx + nc, k, 2 * k), F32),
                        pltpu.VMEM((nx + nc, 1, 2 * k), F32),
                        pltpu.VMEM((nx, k, 2 * k), BF16),
                        pltpu.VMEM((2, math.gcd(nx, 4) * SCAN_BLOCK, 2 * k), F32)],
        compiler_params=_cparams("parallel", "parallel"),
        name="hgrn2_scan",
    )(qx, vx, kkx, kkx, lfx, lfx, gx, qc, vc, kkc, kkc, lfc, lfc, gain,
      jnp.tril(jnp.ones((SCAN_BLOCK, SCAN_BLOCK), BF16)), parity)


def _mix_kernel(n_groups, chunk, u_ref, v_ref, g_ref, b_ref, ws_ref, bias_ref, y_ref, vn_ref):
    vn_ref[...] = (_ln_rows(v_ref[...].astype(F32)) * g_ref[...] + b_ref[...]).astype(BF16)
    gc = v_ref.shape[1] // n_groups
    for cidx in range(v_ref.shape[0] // chunk):
        rows = pl.ds(cidx * chunk, chunk)
        for g in range(n_groups):
            cols = pl.ds(g * gc, gc)
            mixed = jnp.dot(ws_ref[g], vn_ref[rows, cols], preferred_element_type=F32)
            y_ref[rows, cols] = (u_ref[rows, cols].astype(F32) * (mixed + bias_ref[:, cols])).astype(BF16)


def _chunk_mix(uv, v_g, v_b, w_s, bias, rows_per_step):
    m = uv.shape[0]
    bw = uv.shape[1] // 2
    groups, chunk, _ = w_s.shape
    tm = rows_per_step
    return pl.pallas_call(
        functools.partial(_mix_kernel, groups, chunk),
        grid=(m // tm,),
        in_specs=[pl.BlockSpec((tm, bw), lambda i: (i, 0)),
                  pl.BlockSpec((tm, bw), lambda i: (i, 1)),
                  pl.BlockSpec((1, bw), lambda i: (0, 0)),
                  pl.BlockSpec((1, bw), lambda i: (0, 0)),
                  pl.BlockSpec((groups, chunk, chunk), lambda i: (0, 0, 0)),
                  pl.BlockSpec((chunk, bw), lambda i: (0, 0))],
        out_specs=pl.BlockSpec((tm, bw), lambda i: (i, 0)),
        out_shape=jax.ShapeDtypeStruct((m, bw), BF16),
        scratch_shapes=[pltpu.VMEM((tm, bw), BF16)],
        compiler_params=_cparams("parallel"),
        name="chunk_mix",
    )(uv, uv, v_g, v_b, w_s, bias)


def _merge_kernel(n_cast, ya_ref, yb_ref, wa_ref, wb_ref, ga_ref, gb_ref, *refs):
    cast_src, o_ref, cast_dst = refs[:n_cast], refs[n_cast], refs[n_cast + 1:]
    a = jnp.dot(ya_ref[...], wa_ref[...], preferred_element_type=F32)
    bq = jnp.dot(yb_ref[...], wb_ref[...], preferred_element_type=F32)
    o_ref[...] = (ga_ref[...].astype(F32) * a + gb_ref[...].astype(F32) * bq).astype(BF16)
    _do_casts(cast_src, cast_dst)


def _merge(ya, yb, wa, wb, gates, casts=()):
    m, ka = ya.shape
    kb = yb.shape[1]
    d = wa.shape[1]
    tm, tn = min(MXU_TILE, m), min(MXU_TILE, d)
    nd = d // tn
    steps = (m // tm) * nd
    riding = _split_casts(casts, steps)
    c_in, c_out, c_shapes = _cast_specs(riding, steps, lambda i, j: i * nd + j)
    res = pl.pallas_call(
        functools.partial(_merge_kernel, len(riding)),
        grid=(m // tm, nd),
        in_specs=[pl.BlockSpec((tm, ka), lambda i, j: (i, 0)),
                  pl.BlockSpec((tm, kb), lambda i, j: (i, 0)),
                  pl.BlockSpec((ka, tn), lambda i, j: (0, j)),
                  pl.BlockSpec((kb, tn), lambda i, j: (0, j)),
                  pl.BlockSpec((tm, tn), lambda i, j: (i, j)),
                  pl.BlockSpec((tm, tn), lambda i, j: (i, j + nd))] + c_in,
        out_specs=[pl.BlockSpec((tm, tn), lambda i, j: (i, j))] + c_out,
        out_shape=[jax.ShapeDtypeStruct((m, d), BF16)] + c_shapes,
        compiler_params=_cparams("parallel", "arbitrary"),
        name="branch_merge",
    )(ya, yb, wa, wb, gates, gates, *[c.src for c in riding])
    return [res[0]] + _cast_results(casts, steps, res[1:])


def _resid_pos_kernel(alpha, n_row_code, x_ref, w_ref, r_ref, rt_ref, ct_ref, gate_ref, o_ref):
    y = jnp.dot(x_ref[...], w_ref[...], preferred_element_type=F32)
    is_row = (pl.program_id(1) < n_row_code).astype(F32)
    xp = r_ref[...] + is_row * rt_ref[...] + (1.0 - is_row) * ct_ref[...]
    o_ref[...] = alpha * xp.reshape(o_ref.shape) + gate_ref[...] * y


def _out_proj_resid(xin, w, x, row_tab, col_tab, mod3, gate_idx, alpha):
    m, kdim = xin.shape
    b, t, d = x.shape
    half = d // 2
    tm, tn = min(MXU_TILE, m, t), min(MXU_TILE, half)
    assert tm % GRID_W == 0 and half % tn == 0
    per_batch = t // tm
    nd, nh = d // tn, half // tn
    r = tm // GRID_W
    x3 = x.reshape(m // GRID_W, GRID_W, d)
    return pl.pallas_call(
        functools.partial(_resid_pos_kernel, alpha, nh),
        grid=(m // tm, nd),
        in_specs=[pl.BlockSpec((tm, kdim), lambda i, j: (i, 0)),
                  pl.BlockSpec((kdim, tn), lambda i, j: (0, j)),
                  pl.BlockSpec((r, GRID_W, tn), lambda i, j: (i, 0, j)),
                  pl.BlockSpec((r, 1, tn), lambda i, j: (i % per_batch, 0, jnp.minimum(j, nh - 1))),
                  pl.BlockSpec((1, GRID_W, tn), lambda i, j: (0, 0, jnp.maximum(j - nh, 0))),
                  pl.BlockSpec((None, 1, tn), lambda i, j: (i // per_batch, 0, gate_idx * nd + j))],
        out_specs=pl.BlockSpec((tm, tn), lambda i, j: (i, j)),
        out_shape=jax.ShapeDtypeStruct((m, d), F32),
        compiler_params=_cparams("parallel", "arbitrary", vmem=VMEM_LIMIT_WIDE),
        name="out_proj_resid",
    )(xin, w, x3, row_tab, col_tab, mod3)


def _ln_ffn_up_kernel(n_cast, p_ref, g_ref, b_ref, sh_ref, sc_ref, w_ref, *refs):
    cast_src, refs = refs[:n_cast], refs[n_cast:]
    x1_ref, act_ref = refs[0], refs[1]
    cast_dst, h_ref = refs[2:2 + n_cast], refs[2 + n_cast]
    i, j = pl.program_id(0), pl.program_id(1)
    rows = p_ref.shape[0]

    def prepare():
        x = _ln_rows(p_ref[...]) * g_ref[...] + b_ref[...]
        x1_ref[...] = x
        h = _ln_rows(x) * (1.0 + sc_ref[...]) + sh_ref[...]
        h_ref[i % 2, pl.ds(pl.multiple_of(j * rows, rows), rows), :] = h.astype(BF16)
        _do_casts(cast_src, cast_dst)

    @pl.when(i == 0)
    def _():
        prepare()

    @pl.when(i > 0)
    def _():
        y = jnp.dot(h_ref[(i - 1) % 2], w_ref[...], preferred_element_type=F32)
        act_ref[...] = jnp.square(jnp.maximum(y, 0.0)).astype(BF16)
        prepare()


def _ln_ffn_up(pre, g, bvec, mod3, shift_idx, scale_idx, t, w, casts=()):
    m, d = pre.shape
    n = w.shape[1]
    tm, tn = min(MXU_TILE, m, t), min(MXU_TILE, n)
    n_i, n_j = m // tm, n // tn
    rows = tm // n_j
    assert rows * n_j == tm and rows % BF16_SUBLANES == 0
    per_batch = t // tm
    tile = lambda i: jnp.minimum(i, n_i - 1)
    prev = lambda i: jnp.maximum(i - 1, 0)
    steps = n_i * n_j
    slab = pl.BlockSpec((rows, d), lambda i, j: (jnp.minimum(i * n_j + j, steps - 1), 0))
    vec = pl.BlockSpec((1, d), lambda i, j: (0, 0))
    riding = _split_casts(casts, steps)
    c_in, c_out, c_shapes = _cast_specs(riding, steps, lambda i, j: jnp.where(i > 0, (i - 1) * n_j + j, 0))
    res = pl.pallas_call(
        functools.partial(_ln_ffn_up_kernel, len(riding)),
        grid=(n_i + 1, n_j),
        in_specs=[slab, vec, vec,
                  pl.BlockSpec((None, 1, d), lambda i, j: (tile(i) // per_batch, 0, shift_idx)),
                  pl.BlockSpec((None, 1, d), lambda i, j: (tile(i) // per_batch, 0, scale_idx)),
                  pl.BlockSpec((d, tn), lambda i, j: (0, jnp.where(i > 0, j, 0)))] + c_in,
        out_specs=[slab, pl.BlockSpec((tm, tn), lambda i, j: (prev(i), jnp.where(i > 0, j, 0)))] + c_out,
        out_shape=[jax.ShapeDtypeStruct((m, d), F32), jax.ShapeDtypeStruct((m, n), BF16)] + c_shapes,
        scratch_shapes=[pltpu.VMEM((2, tm, d), BF16)],
        compiler_params=_cparams("arbitrary", "arbitrary"),
        name="ln_ffn_up",
    )(pre, g, bvec, mod3, mod3, w, *[c.src for c in riding])
    return [res[0], res[1]] + _cast_results(casts, steps, res[2:])


def _ln_kernel(p_ref, g_ref, b_ref, o_ref):
    o_ref[...] = _ln_rows(p_ref[...]) * g_ref[...] + b_ref[...]


def _ln(pre, g, bvec, tm):
    m, d = pre.shape
    row = pl.BlockSpec((tm, d), lambda i: (i, 0))
    vec = pl.BlockSpec((1, d), lambda i: (0, 0))
    return pl.pallas_call(
        _ln_kernel,
        grid=(m // tm,),
        in_specs=[row, vec, vec],
        out_specs=row,
        out_shape=jax.ShapeDtypeStruct((m, d), F32),
        compiler_params=_cparams("parallel"),
        name="ln_out",
    )(pre, g, bvec)


def _ffn2_kernel(alpha, n_k, a_ref, w_ref, r_ref, gate_ref, o_ref):
    kk = pl.program_id(2)
    last = n_k - 1
    part = lambda: jnp.dot(a_ref[...], w_ref[...], preferred_element_type=F32)
    if last == 0:
        o_ref[...] = alpha * r_ref[...] + gate_ref[...] * part()
        return

    @pl.when(kk == 0)
    def _():
        o_ref[...] = part()

    if last > 1:
        @pl.when(jnp.logical_and(kk > 0, kk < last))
        def _():
            o_ref[...] += part()

    @pl.when(kk == last)
    def _():
        o_ref[...] = alpha * r_ref[...] + gate_ref[...] * (o_ref[...] + part())


def _ffn2_resid(a, w, resid, mod3, gate_idx, alpha, t):
    m, f = a.shape
    d = w.shape[1]
    tm, tn, tk = min(MXU_TILE, m, t), min(MXU_TILE, d), min(4 * MXU_TILE, f)
    per_batch = t // tm
    nd = d // tn
    return pl.pallas_call(
        functools.partial(_ffn2_kernel, alpha, f // tk),
        grid=(m // tm, nd, f // tk),
        in_specs=[pl.BlockSpec((tm, tk), lambda i, j, kk: (i, kk)),
                  pl.BlockSpec((tk, tn), lambda i, j, kk: (kk, j)),
                  pl.BlockSpec((tm, tn), lambda i, j, kk: (i, j)),
                  pl.BlockSpec((None, 1, tn), lambda i, j, kk: (i // per_batch, 0, gate_idx * nd + j))],
        out_specs=pl.BlockSpec((tm, tn), lambda i, j, kk: (i, j)),
        out_shape=jax.ShapeDtypeStruct((m, d), F32),
        compiler_params=_cparams("parallel", "parallel", "arbitrary", vmem=VMEM_LIMIT_WIDE),
        name="ffn_down_resid",
    )(a, w, resid, mod3)


def _pos_tables(rows, cols, dim):
    quarter = dim // 4
    omega = 1.0 / (POS_BASE ** (jnp.arange(quarter, dtype=F32) / quarter))

    def emb(p):
        a = p[:, None] * omega[None, :]
        return jnp.concatenate([jnp.sin(a), jnp.cos(a)], axis=-1)

    return (emb(jnp.arange(rows, dtype=F32))[:, None, :], emb(jnp.arange(cols, dtype=F32))[None, :, :])


def kernel(x, c, ctx, c_ctx, w_ada, b_ada, w_in, lb_logits, a_norm_g, w_proj_a, v_norm_g, v_norm_b,
           w_s, b_s, w_proj_b, w_out, ln1_g, ln1_b, w_ff1, w_ff2, ln2_g, ln2_b):
    depth = w_ada.shape[0]
    assert depth == 1, "only the single-layer block is implemented"
    b, t, d = x.shape
    tc = ctx.shape[1]
    a_fdim = lb_logits.shape[-1]
    a_width = w_proj_a.shape[1]
    b_width = w_proj_b.shape[1]
    assert a_fdim == a_width and a_norm_g.shape[-1] == HEAD_DIM
    assert t % GRID_W == 0 and t % SCAN_BLOCK == 0 and tc % SCAN_BLOCK == 0
    assert w_s.shape[2] == SCAN_BLOCK or t % w_s.shape[2] == 0
    alpha = (2.0 * depth) ** 0.25
    m = b * t

    sizes = (a_fdim, a_width, a_fdim, a_fdim, a_width, b_width, b_width, d, d)
    offs = [0]
    for s in sizes:
        offs.append(offs[-1] + s)
    assert offs[-1] == w_in.shape[-1]

    w_in0 = w_in[0]
    seg = lambda first, last: _Cast(w_in0, offs[first], offs[last + 1] - offs[first])
    wq_b = w_in0[:, :offs[1]].astype(BF16)
    ws_b = w_s[0].astype(BF16)
    lb_all = jnp.cumsum(jax.nn.softmax(lb_logits.astype(F32), axis=1), axis=1)
    lb_row = jnp.concatenate([lb_all[0, 0], lb_all[1, 0]])[None, :]
    mix_bias = jnp.repeat(b_s[0].T, b_width // w_s.shape[1], axis=1)
    row_tab, col_tab = _pos_tables(t // GRID_W, GRID_W, d)

    n_rows = -(-(b + 1) // 8) * 8
    cc = jnp.concatenate([c, c_ctx[None, :], jnp.zeros((n_rows - b - 1, d), F32)], axis=0)
    mod = _modulation(cc, w_ada[0], b_ada[0][None, :])
    mod3 = mod.reshape(n_rows, 1, N_MOD * d)

    h = _adaln_pos(x, row_tab, col_tab, mod3, rows_per_step=min(4, t // GRID_W))
    hc = _adaln_ctx(ctx.reshape(b * tc, d), mod3, b, tm=min(256, b * tc))

    scale = HEAD_DIM ** -0.5
    ep_q = functools.partial(_ep_silu, scale)
    ep_g = functools.partial(_ep_silu, 1.0)

    qx, wi_b = _in_proj(h, wq_b, ep_q, [BF16], casts=(seg(1, 1),), name="in_proj_q")
    vx, wf_b = _in_proj(h, wi_b, _ep_cast, [BF16], casts=(seg(2, 3),), name="in_proj_i")
    kkx, lfx, wg_b = _in_proj(h, wf_b, _ep_decay, [BF16, F32], extra=(lb_row,), casts=(seg(4, 4),),
                              name="in_proj_f")
    gx, wuv_b = _in_proj(h, wg_b, ep_g, [BF16], casts=(seg(5, 6),), name="in_proj_g")
    uv, wgate_b = _in_proj(h, wuv_b, _ep_cast, [BF16], casts=(seg(7, 8),), name="in_proj_uv")
    gates, wa_b, wb_b, w1_b = _in_proj(
        h, wgate_b, _ep_sigmoid, [BF16],
        casts=(_Cast(w_proj_a[0], 0, d), _Cast(w_proj_b[0], 0, d), _Cast(w_ff1[0], 0, w_ff1.shape[-1])),
        name="in_proj_gates")

    (qc,) = _in_proj(hc, wq_b, ep_q, [BF16], name="in_proj_q_ctx")
    (vc,) = _in_proj(hc, wi_b, _ep_cast, [BF16], name="in_proj_i_ctx")
    kkc, lfc = _in_proj(hc, wf_b, _ep_decay, [BF16, F32], extra=(lb_row,), name="in_proj_f_ctx")

    ya = _hgrn2_scan((qx, vx, kkx, lfx, gx), (qc, vc, kkc, lfc), a_norm_g[0][None, :], b, t, tc)
    yb = _chunk_mix(uv, v_norm_g[0][None, :], v_norm_b[0][None, :], ws_b, mix_bias,
                    rows_per_step=min(2 * w_s.shape[2], t))

    merged, wo_b = _merge(ya, yb, wa_b, wb_b, gates, casts=(_Cast(w_out[0], 0, d),))
    pre1 = _out_proj_resid(merged, wo_b, x, row_tab, col_tab, mod3, 2, alpha)
    x1, act, w2_b = _ln_ffn_up(pre1, ln1_g[0][None, :], ln1_b[0][None, :], mod3, 3, 4, t, w1_b,
                               casts=(_Cast(w_ff2[0], 0, d),))
    pre2 = _ffn2_resid(act, w2_b, x1, mod3, 5, alpha, t)
    out = _ln(pre2, ln2_g[0][None, :], ln2_b[0][None, :], tm=min(256, m))
    return out.reshape(b, t, d)
```

```python
import functools
import math
from typing import NamedTuple

import jax
import jax.numpy as jnp
from jax import lax
from jax.experimental import pallas as pl
from jax.experimental.pallas import tpu as pltpu

F32 = jnp.float32
BF16 = jnp.bfloat16

GRID_W = 64
LN_EPS = 1e-6
POS_BASE = 10000.0
N_MOD = 6
HEAD_DIM = 128
SCAN_BLOCK = 128
MXU_TILE = 1024
CAST_COLS = 4096
BF16_SUBLANES = 16
LN_ROWS = BF16_SUBLANES
VMEM_LIMIT = 56 * 1024 * 1024
VMEM_LIMIT_WIDE = 60 * 1024 * 1024


def _cparams(*sem, vmem=VMEM_LIMIT):
    return pltpu.CompilerParams(dimension_semantics=sem, vmem_limit_bytes=vmem)


def _sigmoid(x):
    return 0.5 * jnp.tanh(0.5 * x) + 0.5


def _ln_rows(x):
    mu = jnp.mean(x, axis=-1, keepdims=True)
    xc = x - mu
    return xc * lax.rsqrt(jnp.mean(xc * xc, axis=-1, keepdims=True) + LN_EPS)


def _mod_kernel(c_ref, w_ref, b_ref, o_ref):
    a = c_ref[...]
    a = a * _sigmoid(a)
    o_ref[...] = jnp.dot(a.astype(BF16), w_ref[...].astype(BF16),
                         preferred_element_type=F32) + b_ref[...]


def _modulation(cc, w_ada, b_ada):
    rows, d = cc.shape
    n = w_ada.shape[1]
    tn = min(512, n)
    return pl.pallas_call(
        _mod_kernel,
        grid=(n // tn,),
        in_specs=[pl.BlockSpec((rows, d), lambda j: (0, 0)),
                  pl.BlockSpec((d, tn), lambda j: (0, j)),
                  pl.BlockSpec((1, tn), lambda j: (0, j))],
        out_specs=pl.BlockSpec((rows, tn), lambda j: (0, j)),
        out_shape=jax.ShapeDtypeStruct((rows, n), F32),
        compiler_params=_cparams("arbitrary"),
        name="modulation",
    )(cc, w_ada, b_ada)


def _adaln_pos_kernel(x_ref, rt_ref, ct_ref, sh_ref, sc_ref, h_ref):
    half = rt_ref.shape[-1]
    d = 2 * half
    sh = sh_ref[...]
    sc = 1.0 + sc_ref[...]

    def rows_body(r, g):
        rows = pl.ds(pl.multiple_of(g * LN_ROWS, LN_ROWS), LN_ROWS)
        xa = x_ref[r, rows, :half] + rt_ref[r]
        xb = x_ref[r, rows, half:] + ct_ref[0, rows, :]
        mu = (jnp.sum(xa, axis=-1, keepdims=True) + jnp.sum(xb, axis=-1, keepdims=True)) * (1.0 / d)
        xa = xa - mu
        xb = xb - mu
        var = (jnp.sum(xa * xa, axis=-1, keepdims=True) + jnp.sum(xb * xb, axis=-1, keepdims=True)) * (1.0 / d)
        inv = lax.rsqrt(var + LN_EPS)
        h_ref[r, rows, :half] = (xa * inv * sc[:, :half] + sh[:, :half]).astype(BF16)
        h_ref[r, rows, half:] = (xb * inv * sc[:, half:] + sh[:, half:]).astype(BF16)

    groups = x_ref.shape[1] // LN_ROWS
    for r in range(x_ref.shape[0]):
        lax.fori_loop(0, groups, lambda g, c, r=r: (rows_body(r, g), c)[1], 0, unroll=True)


def _adaln_pos(x, row_tab, col_tab, mod3, rows_per_step):
    b, t, d = x.shape
    gr = t // GRID_W
    r = rows_per_step
    x4 = x.reshape(b, gr, GRID_W, d)
    blk = pl.BlockSpec((None, r, GRID_W, d), lambda i, j: (i, j, 0, 0))
    h = pl.pallas_call(
        _adaln_pos_kernel,
        grid=(b, gr // r),
        in_specs=[blk,
                  pl.BlockSpec((r, 1, d // 2), lambda i, j: (j, 0, 0)),
                  pl.BlockSpec((1, GRID_W, d // 2), lambda i, j: (0, 0, 0)),
                  pl.BlockSpec((None, 1, d), lambda i, j: (i, 0, 0)),
                  pl.BlockSpec((None, 1, d), lambda i, j: (i, 0, 1))],
        out_specs=blk,
        out_shape=jax.ShapeDtypeStruct(x4.shape, BF16),
        compiler_params=_cparams("parallel", "parallel"),
        name="adaln_pos",
    )(x4, row_tab, col_tab, mod3, mod3)
    return h.reshape(b * t, d)


def _adaln_rows_kernel(x_ref, sh_ref, sc_ref, h_ref):
    y = _ln_rows(x_ref[...])
    h_ref[...] = (y * (1.0 + sc_ref[...]) + sh_ref[...]).astype(BF16)


def _adaln_ctx(xc, mod3, ctx_row, tm):
    m, d = xc.shape
    return pl.pallas_call(
        _adaln_rows_kernel,
        grid=(m // tm,),
        in_specs=[pl.BlockSpec((tm, d), lambda i: (i, 0)),
                  pl.BlockSpec((None, 1, d), lambda i: (ctx_row, 0, 0)),
                  pl.BlockSpec((None, 1, d), lambda i: (ctx_row, 0, 1))],
        out_specs=pl.BlockSpec((tm, d), lambda i: (i, 0)),
        out_shape=jax.ShapeDtypeStruct((m, d), BF16),
        compiler_params=_cparams("parallel"),
        name="adaln_ctx",
    )(xc, mod3, mod3)


def _ep_silu(scale, acc, o_ref):
    hz = 0.5 * acc
    o_ref[...] = ((hz if scale == 1.0 else hz * scale) * (1.0 + jnp.tanh(hz))).astype(o_ref.dtype)


def _ep_cast(acc, o_ref):
    o_ref[...] = acc.astype(o_ref.dtype)


def _ep_sigmoid(acc, o_ref):
    o_ref[...] = _sigmoid(acc).astype(o_ref.dtype)


def _ep_decay(acc, lb_ref, k_ref, lf_ref):
    lb = lb_ref[...]
    c = 0.5 - 0.5 * lb
    ct = c * jnp.tanh(0.5 * acc)
    k_ref[...] = (c - ct).astype(k_ref.dtype)
    lf_ref[...] = jnp.log2((0.5 + 0.5 * lb) + ct)


class _Cast(NamedTuple):
    src: jax.Array
    col0: int
    width: int


def _cast_blocks(cast, steps):
    cw = math.gcd(cast.col0, cast.width, CAST_COLS)
    ncb = cast.width // cw
    return (cast.src.shape[0] * ncb) // steps, cw, ncb


def _cast_rides(cast, steps):
    rb, _, ncb = _cast_blocks(cast, steps)
    rows = cast.src.shape[0]
    return rb * steps == rows * ncb and rb % BF16_SUBLANES == 0 and rows % rb == 0


def _split_casts(casts, steps):
    return [c for c in casts if _cast_rides(c, steps)]


def _cast_results(casts, steps, ridden):
    ridden = list(ridden)
    return [ridden.pop(0) if _cast_rides(c, steps) else c.src[:, c.col0:c.col0 + c.width].astype(BF16)
            for c in casts]


def _cast_specs(casts, steps, step_of):
    in_specs, out_specs, out_shapes = [], [], []
    for cast in casts:
        src, col0, width = cast
        rows = src.shape[0]
        rb, cw, ncb = _cast_blocks(cast, steps)
        coff = col0 // cw
        in_specs.append(pl.BlockSpec(
            (rb, cw), lambda i, j, ncb=ncb, coff=coff: (step_of(i, j) // ncb, coff + step_of(i, j) % ncb)))
        out_specs.append(pl.BlockSpec(
            (rb, cw), lambda i, j, ncb=ncb: (step_of(i, j) // ncb, step_of(i, j) % ncb)))
        out_shapes.append(jax.ShapeDtypeStruct((rows, width), BF16))
    return in_specs, out_specs, out_shapes


def _do_casts(src_refs, dst_refs):
    for s_ref, d_ref in zip(src_refs, dst_refs, strict=True):
        d_ref[...] = s_ref[...].astype(BF16)


def _proj_kernel(epilogue, n_extra, n_cast, x_ref, w_ref, *refs):
    extras, refs = refs[:n_extra], refs[n_extra:]
    cast_src, refs = refs[:n_cast], refs[n_cast:]
    outs, cast_dst = refs[:len(refs) - n_cast], refs[len(refs) - n_cast:]
    acc = jnp.dot(x_ref[...], w_ref[...], preferred_element_type=F32)
    epilogue(acc, *extras, *outs)
    _do_casts(cast_src, cast_dst)


def _in_proj(h, w, epilogue, out_dtypes, extra=(), casts=(), name="in_proj"):
    m, d = h.shape
    width = w.shape[1]
    tm = min(MXU_TILE, m)
    tn = min(MXU_TILE, width)
    n_i, n_j = m // tm, width // tn
    out_blk = pl.BlockSpec((tm, tn), lambda i, j: (i, j))
    riding = _split_casts(casts, n_i * n_j)
    c_in, c_out, c_shapes = _cast_specs(riding, n_i * n_j, lambda i, j: i * n_j + j)
    res = pl.pallas_call(
        functools.partial(_proj_kernel, epilogue, len(extra), len(riding)),
        grid=(n_i, n_j),
        in_specs=[pl.BlockSpec((tm, d), lambda i, j: (i, 0)),
                  pl.BlockSpec((d, tn), lambda i, j: (0, j))]
                 + [pl.BlockSpec((1, tn), lambda i, j: (0, j)) for _ in extra] + c_in,
        out_specs=[out_blk for _ in out_dtypes] + c_out,
        out_shape=[jax.ShapeDtypeStruct((m, width), dt) for dt in out_dtypes] + c_shapes,
        compiler_params=_cparams("parallel", "arbitrary"),
        name=name,
    )(h, w, *extra, *[c.src for c in riding])
    n_out = len(out_dtypes)
    return list(res[:n_out]) + _cast_results(casts, n_i * n_j, res[n_out:])


def _mid_rows(b, m):
    c, w = b.shape
    assert m in (1, 2, 4)
    if m == 1:
        odd = (lax.broadcasted_iota(jnp.int32, (c, w), 0) & 1) == 1
        return jnp.where(odd, pltpu.roll(b, 1, 0), b)
    b3 = b.reshape(c // 8, 8, w)
    if m == 4:
        return jnp.broadcast_to(b3[:, 3:4, :], b3.shape).reshape(c, w)
    low = lax.broadcasted_iota(jnp.int32, b3.shape, 1) < 4
    return jnp.where(low, jnp.broadcast_to(b3[:, 1:2, :], b3.shape),
                     jnp.broadcast_to(b3[:, 5:6, :], b3.shape)).reshape(c, w)


def _block_sums(tri, lf, lb):
    k = lf.shape[1]
    parts = []
    for a in (lf, lb):
        hi = a.astype(BF16)
        parts += [hi, (a - hi.astype(F32)).astype(BF16)]
    cs = jnp.dot(tri, jnp.concatenate(parts, axis=1), preferred_element_type=F32)
    return jnp.concatenate([cs[:, :k] + cs[:, k:2 * k], cs[:, 2 * k:3 * k] + cs[:, 3 * k:]], axis=1)


def _state_terms(v, kf, kb, bcat, lb):
    c, k = lb.shape
    tot = bcat[c - 1:c]
    ks = jnp.concatenate([kf * jnp.exp2(tot[:, :k] - bcat[:, :k]),
                          kb * jnp.exp2(bcat[:, k:] - lb)], axis=1)
    return pl.dot(v, ks.astype(BF16), trans_a=True), jnp.exp2(tot)


def _level_products(q, kf, kb, bcat, lb, par_ref):
    c, k = lb.shape
    xf, xb = bcat[:, :k], bcat[:, k:] - lb
    prods = []
    m, level = 1, 0
    while m < c:
        if m % 8 == 0:
            split = lambda a: a.reshape(c // (2 * m), 2, m, a.shape[-1])
            join = lambda even, odd: jnp.concatenate([even, odd], axis=1).reshape(c, k)
            xf4, xb4, kf4, kb4 = split(xf), split(xb), split(kf), split(kb)
            mid = split(bcat)[:, 0:1, m - 1:m, :]
            mf, mb = mid[..., :k], mid[..., k:]
            eq = join(mb - xb4[:, 0:1], xf4[:, 1:2] - mf)
            ek = join(mf - xf4[:, 0:1], xb4[:, 1:2] - mb)
            ksel = join(kf4[:, 0:1], kb4[:, 1:2])
        else:
            mid = _mid_rows(bcat, m)
            df = xf - mid[:, :k]
            db = mid[:, k:] - xb
            eq = jnp.minimum(df, db)
            ek = -jnp.maximum(df, db)
            ksel = jnp.where(par_ref[level] > 0, kb, kf)
        prods.append(pl.dot(q * jnp.exp2(eq), ksel * jnp.exp2(ek), trans_b=True))
        m *= 2
        level += 1
    prods.append(pl.dot(q, kf + kb, trans_b=True))
    return prods


def _assemble(prods):
    c = prods[0].shape[0]
    x = (lax.broadcasted_iota(jnp.int32, (c, c), 0) ^ lax.broadcasted_iota(jnp.int32, (c, c), 1))
    att = prods[-2]
    m = c // 2
    for a in reversed(prods[:-2]):
        att = jnp.where(x < m, a, att)
        m //= 2
    return jnp.where(x < 1, prods[-1], att).astype(BF16)


def _entry_factors(q, bcat, lb):
    c, k = lb.shape
    xb = bcat[:, k:] - lb
    return jnp.concatenate([q * jnp.exp2(bcat[:, :k]),
                            q * jnp.exp2(bcat[c - 1:c, k:] - xb)], axis=1).astype(BF16)


def _scan_kernel(nx, nc,
                 qx, vx, kfx, kbx, lfx, lbx, gx, qc, vc, kfc, kbc, lfc, lbc, gain_ref, tri_ref, par_ref,
                 y_ref, oi_ref, qi_ref, u_ref, d_ref, s_ref, b_ref):
    c = SCAN_BLOCK
    k = HEAD_DIM
    f32 = lambda ref, rows: ref[rows, :].astype(F32)

    ctx_rows = [pl.ds(j * c, c) for j in range(nc)]
    ctx_sums = [_block_sums(tri_ref[...], lfc[rows, :], lbc[rows, :]) for rows in ctx_rows]

    group = math.gcd(nx, 4)
    n_groups = nx // group

    def sums(g, slot):
        for i in range(group):
            rows = pl.ds(pl.multiple_of((g * group + i) * c, c), c)
            b_ref[slot, pl.ds(i * c, c), :] = _block_sums(tri_ref[...], lfx[rows, :], lbx[rows, :])

    sums(0, 0)
    for j, (rows, bcat) in enumerate(zip(ctx_rows, ctx_sums)):
        u_t, dec = _state_terms(vc[rows, :], f32(kfc, rows), f32(kbc, rows), bcat, lbc[rows, :])
        u_ref[j] = u_t
        d_ref[j] = dec

    def do_group(p, slot):
        sums(jnp.minimum(p + 1, n_groups - 1), 1 - slot)
        rows = [pl.ds(pl.multiple_of((p * group + i) * c, c), c) for i in range(group)]
        bcats = [b_ref[slot, pl.ds(i * c, c), :] for i in range(group)]
        qkk = [(f32(qx, rw), f32(kfx, rw), f32(kbx, rw)) for rw in rows]
        prods = [_level_products(q, kf, kb, bcat, lbx[rw, :], par_ref)
                 for (q, kf, kb), rw, bcat in zip(qkk, rows, bcats)]
        for i, ((_, kf, kb), rw, bcat) in enumerate(zip(qkk, rows, bcats)):
            u_t, dec = _state_terms(vx[rw, :], kf, kb, bcat, lbx[rw, :])
            u_ref[nc + p * group + i] = u_t
            d_ref[nc + p * group + i] = dec
        outs = [jnp.dot(_assemble(pr), vx[rw, :], preferred_element_type=F32) for pr, rw in zip(prods, rows)]
        for (q, _, _), rw, bcat in zip(qkk, rows, bcats):
            qi_ref[rw, :] = _entry_factors(q, bcat, lbx[rw, :])
        for o, rw in zip(outs, rows):
            oi_ref[rw, :] = o

    def pair_body(pp, carry):
        do_group(2 * pp, 0)
        do_group(2 * pp + 1, 1)
        return carry

    lax.fori_loop(0, n_groups // 2, pair_body, 0)
    if n_groups % 2:
        do_group(n_groups - 1, 0)

    sf = jnp.zeros((k, k), F32)
    sb = jnp.zeros((k, k), F32)
    for j in range(nc):
        sf = d_ref[j][:, :k] * sf + u_ref[j][:, :k]
        jb = nc - 1 - j
        sb = d_ref[jb][:, k:] * sb + u_ref[jb][:, k:]

    def sweep_body(i, carry):
        sf, sb = carry
        s_ref[i, :, :k] = sf.astype(BF16)
        sf = d_ref[nc + i][:, :k] * sf + u_ref[nc + i][:, :k]
        j = nx - 1 - i
        s_ref[j, :, k:] = sb.astype(BF16)
        sb = d_ref[nc + j][:, k:] * sb + u_ref[nc + j][:, k:]
        return sf, sb

    lax.fori_loop(0, nx, sweep_body, (sf, sb))

    gain = gain_ref[...]

    def out_body(n, carry):
        rows = pl.ds(pl.multiple_of(n * c, c), c)
        o = oi_ref[rows, :] + pl.dot(qi_ref[rows, :], s_ref[n], trans_b=True)
        o = o * lax.rsqrt(jnp.mean(o * o, axis=-1, keepdims=True) + LN_EPS) * gain
        y_ref[rows, :] = (o * gx[rows, :].astype(F32)).astype(BF16)
        return carry

    lax.fori_loop(0, nx, out_body, 0, unroll=math.gcd(nx, 16))


def _hgrn2_scan(px, pc, gain, b, t, tc):
    qx, vx, kkx, lfx, gx = px
    qc, vc, kkc, lfc = pc
    k = HEAD_DIM
    heads = qx.shape[1] // k
    nx, nc = t // SCAN_BLOCK, tc // SCAN_BLOCK
    n_levels = SCAN_BLOCK.bit_length() - 1
    bits = (jnp.arange(SCAN_BLOCK)[None, :] >> jnp.arange(n_levels)[:, None]) & 1
    parity = jnp.broadcast_to(bits[:, :, None], (n_levels, SCAN_BLOCK, k)).astype(F32)
    fwd = lambda rows: pl.BlockSpec((rows, k), lambda i, h: (i, h))
    bwd = lambda rows: pl.BlockSpec((rows, k), lambda i, h: (i, h + heads))
    return pl.pallas_call(
        functools.partial(_scan_kernel, nx, nc),
        grid=(b, heads),
        in_specs=[fwd(t), fwd(t), fwd(t), bwd(t), fwd(t), bwd(t), fwd(t),
                  fwd(tc), fwd(tc), fwd(tc), bwd(tc), fwd(tc), bwd(tc),
                  pl.BlockSpec((1, k), lambda i, h: (0, 0)),
                  pl.BlockSpec((SCAN_BLOCK, SCAN_BLOCK), lambda i, h: (0, 0)),
                  pl.BlockSpec(parity.shape, lambda i, h: (0, 0, 0))],
        out_specs=fwd(t),
        out_shape=jax.ShapeDtypeStruct((b * t, heads * k), BF16),
        scratch_shapes=[pltpu.VMEM((t, k), F32),
                        pltpu.VMEM((t, 2 * k), BF16),
                        pltpu.VMEM((nx + nc, k, 2 * k), F32),
                        pltpu.VMEM((nx + nc, 1, 2 * k), F32),
                        pltpu.VMEM((nx, k, 2 * k), BF16),
                        pltpu.VMEM((2, math.gcd(nx, 4) * SCAN_BLOCK, 2 * k), F32)],
        compiler_params=_cparams("parallel", "parallel"),
        name="hgrn2_scan",
    )(qx, vx, kkx, kkx, lfx, lfx, gx, qc, vc, kkc, kkc, lfc, lfc, gain,
      jnp.tril(jnp.ones((SCAN_BLOCK, SCAN_BLOCK), BF16)), parity)


def _mix_kernel(n_groups, chunk, u_ref, v_ref, g_ref, b_ref, ws_ref, bias_ref, y_ref, vn_ref):
    gain, bias = g_ref[...], b_ref[...]
    for r0 in range(0, v_ref.shape[0], LN_ROWS):
        rows = pl.ds(r0, LN_ROWS)
        vn_ref[rows, :] = (_ln_rows(v_ref[rows, :].astype(F32)) * gain + bias).astype(BF16)
    gc = v_ref.shape[1] // n_groups
    for cidx in range(v_ref.shape[0] // chunk):
        rows = pl.ds(cidx * chunk, chunk)
        for g in range(n_groups):
            cols = pl.ds(g * gc, gc)
            mixed = jnp.dot(ws_ref[g], vn_ref[rows, cols], preferred_element_type=F32)
            y_ref[rows, cols] = (u_ref[rows, cols].astype(F32) * (mixed + bias_ref[:, cols])).astype(BF16)


def _chunk_mix(uv, v_g, v_b, w_s, bias, rows_per_step):
    m = uv.shape[0]
    bw = uv.shape[1] // 2
    groups, chunk, _ = w_s.shape
    tm = rows_per_step
    return pl.pallas_call(
        functools.partial(_mix_kernel, groups, chunk),
        grid=(m // tm,),
        in_specs=[pl.BlockSpec((tm, bw), lambda i: (i, 0)),
                  pl.BlockSpec((tm, bw), lambda i: (i, 1)),
                  pl.BlockSpec((1, bw), lambda i: (0, 0)),
                  pl.BlockSpec((1, bw), lambda i: (0, 0)),
                  pl.BlockSpec((groups, chunk, chunk), lambda i: (0, 0, 0)),
                  pl.BlockSpec((chunk, bw), lambda i: (0, 0))],
        out_specs=pl.BlockSpec((tm, bw), lambda i: (i, 0)),
        out_shape=jax.ShapeDtypeStruct((m, bw), BF16),
        scratch_shapes=[pltpu.VMEM((tm, bw), BF16)],
        compiler_params=_cparams("parallel"),
        name="chunk_mix",
    )(uv, uv, v_g, v_b, w_s, bias)


def _merge_kernel(n_cast, ya_ref, yb_ref, wa_ref, wb_ref, ga_ref, gb_ref, *refs):
    cast_src, o_ref, cast_dst = refs[:n_cast], refs[n_cast], refs[n_cast + 1:]
    a = jnp.dot(ya_ref[...], wa_ref[...], preferred_element_type=F32)
    bq = jnp.dot(yb_ref[...], wb_ref[...], preferred_element_type=F32)
    o_ref[...] = (ga_ref[...].astype(F32) * a + gb_ref[...].astype(F32) * bq).astype(BF16)
    _do_casts(cast_src, cast_dst)


def _merge(ya, yb, wa, wb, gates, casts=()):
    m, ka = ya.shape
    kb = yb.shape[1]
    d = wa.shape[1]
    tm, tn = min(MXU_TILE, m), min(MXU_TILE, d)
    nd = d // tn
    steps = (m // tm) * nd
    riding = _split_casts(casts, steps)
    c_in, c_out, c_shapes = _cast_specs(riding, steps, lambda i, j: i * nd + j)
    res = pl.pallas_call(
        functools.partial(_merge_kernel, len(riding)),
        grid=(m // tm, nd),
        in_specs=[pl.BlockSpec((tm, ka), lambda i, j: (i, 0)),
                  pl.BlockSpec((tm, kb), lambda i, j: (i, 0)),
                  pl.BlockSpec((ka, tn), lambda i, j: (0, j)),
                  pl.BlockSpec((kb, tn), lambda i, j: (0, j)),
                  pl.BlockSpec((tm, tn), lambda i, j: (i, j)),
                  pl.BlockSpec((tm, tn), lambda i, j: (i, j + nd))] + c_in,
        out_specs=[pl.BlockSpec((tm, tn), lambda i, j: (i, j))] + c_out,
        out_shape=[jax.ShapeDtypeStruct((m, d), BF16)] + c_shapes,
        compiler_params=_cparams("parallel", "arbitrary"),
        name="branch_merge",
    )(ya, yb, wa, wb, gates, gates, *[c.src for c in riding])
    return [res[0]] + _cast_results(casts, steps, res[1:])


def _resid_pos_kernel(alpha, n_row_code, x_ref, w_ref, r_ref, rt_ref, ct_ref, gate_ref, o_ref):
    y = jnp.dot(x_ref[...], w_ref[...], preferred_element_type=F32)
    is_row = (pl.program_id(1) < n_row_code).astype(F32)
    xp = r_ref[...] + is_row * rt_ref[...] + (1.0 - is_row) * ct_ref[...]
    o_ref[...] = alpha * xp.reshape(o_ref.shape) + gate_ref[...] * y


def _out_proj_resid(xin, w, x, row_tab, col_tab, mod3, gate_idx, alpha):
    m, kdim = xin.shape
    b, t, d = x.shape
    half = d // 2
    tm, tn = min(MXU_TILE, m, t), min(MXU_TILE, half)
    assert tm % GRID_W == 0 and half % tn == 0
    per_batch = t // tm
    nd, nh = d // tn, half // tn
    r = tm // GRID_W
    x3 = x.reshape(m // GRID_W, GRID_W, d)
    return pl.pallas_call(
        functools.partial(_resid_pos_kernel, alpha, nh),
        grid=(m // tm, nd),
        in_specs=[pl.BlockSpec((tm, kdim), lambda i, j: (i, 0)),
                  pl.BlockSpec((kdim, tn), lambda i, j: (0, j)),
                  pl.BlockSpec((r, GRID_W, tn), lambda i, j: (i, 0, j)),
                  pl.BlockSpec((r, 1, tn), lambda i, j: (i % per_batch, 0, jnp.minimum(j, nh - 1))),
                  pl.BlockSpec((1, GRID_W, tn), lambda i, j: (0, 0, jnp.maximum(j - nh, 0))),
                  pl.BlockSpec((None, 1, tn), lambda i, j: (i // per_batch, 0, gate_idx * nd + j))],
        out_specs=pl.BlockSpec((tm, tn), lambda i, j: (i, j)),
        out_shape=jax.ShapeDtypeStruct((m, d), F32),
        compiler_params=_cparams("parallel", "arbitrary", vmem=VMEM_LIMIT_WIDE),
        name="out_proj_resid",
    )(xin, w, x3, row_tab, col_tab, mod3)


def _ln_ffn_up_kernel(n_cast, p_ref, g_ref, b_ref, sh_ref, sc_ref, w_ref, *refs):
    cast_src, refs = refs[:n_cast], refs[n_cast:]
    x1_ref, act_ref = refs[0], refs[1]
    cast_dst, h_ref = refs[2:2 + n_cast], refs[2 + n_cast]
    i, j = pl.program_id(0), pl.program_id(1)
    rows = p_ref.shape[0]

    def prepare():
        g, b, sc, sh = g_ref[...], b_ref[...], 1.0 + sc_ref[...], sh_ref[...]
        for r0 in range(0, rows, LN_ROWS):
            x = _ln_rows(p_ref[pl.ds(r0, LN_ROWS), :]) * g + b
            x1_ref[pl.ds(r0, LN_ROWS), :] = x
            h = _ln_rows(x) * sc + sh
            h_ref[i % 2, pl.ds(pl.multiple_of(j * rows + r0, LN_ROWS), LN_ROWS), :] = h.astype(BF16)
        _do_casts(cast_src, cast_dst)

    @pl.when(i == 0)
    def _():
        prepare()

    @pl.when(i > 0)
    def _():
        y = jnp.dot(h_ref[(i - 1) % 2], w_ref[...], preferred_element_type=F32)
        act_ref[...] = jnp.square(jnp.maximum(y, 0.0)).astype(BF16)
        prepare()


def _ln_ffn_up(pre, g, bvec, mod3, shift_idx, scale_idx, t, w, casts=()):
    m, d = pre.shape
    n = w.shape[1]
    tm, tn = min(MXU_TILE, m, t), min(MXU_TILE, n)
    n_i, n_j = m // tm, n // tn
    rows = tm // n_j
    assert rows * n_j == tm and rows % BF16_SUBLANES == 0
    per_batch = t // tm
    tile = lambda i: jnp.minimum(i, n_i - 1)
    prev = lambda i: jnp.maximum(i - 1, 0)
    steps = n_i * n_j
    slab = pl.BlockSpec((rows, d), lambda i, j: (jnp.minimum(i * n_j + j, steps - 1), 0))
    vec = pl.BlockSpec((1, d), lambda i, j: (0, 0))
    riding = _split_casts(casts, steps)
    c_in, c_out, c_shapes = _cast_specs(riding, steps, lambda i, j: jnp.where(i > 0, (i - 1) * n_j + j, 0))
    res = pl.pallas_call(
        functools.partial(_ln_ffn_up_kernel, len(riding)),
        grid=(n_i + 1, n_j),
        in_specs=[slab, vec, vec,
                  pl.BlockSpec((None, 1, d), lambda i, j: (tile(i) // per_batch, 0, shift_idx)),
                  pl.BlockSpec((None, 1, d), lambda i, j: (tile(i) // per_batch, 0, scale_idx)),
                  pl.BlockSpec((d, tn), lambda i, j: (0, jnp.where(i > 0, j, 0)))] + c_in,
        out_specs=[slab, pl.BlockSpec((tm, tn), lambda i, j: (prev(i), jnp.where(i > 0, j, 0)))] + c_out,
        out_shape=[jax.ShapeDtypeStruct((m, d), F32), jax.ShapeDtypeStruct((m, n), BF16)] + c_shapes,
        scratch_shapes=[pltpu.VMEM((2, tm, d), BF16)],
        compiler_params=_cparams("arbitrary", "arbitrary"),
        name="ln_ffn_up",
    )(pre, g, bvec, mod3, mod3, w, *[c.src for c in riding])
    return [res[0], res[1]] + _cast_results(casts, steps, res[2:])


def _ln_kernel(p_ref, g_ref, b_ref, o_ref):
    g, b = g_ref[...], b_ref[...]

    def rows_body(i, c):
        rows = pl.ds(pl.multiple_of(i * LN_ROWS, LN_ROWS), LN_ROWS)
        o_ref[rows, :] = _ln_rows(p_ref[rows, :]) * g + b
        return c

    lax.fori_loop(0, p_ref.shape[0] // LN_ROWS, rows_body, 0, unroll=True)


def _ln(pre, g, bvec, tm):
    m, d = pre.shape
    row = pl.BlockSpec((tm, d), lambda i: (i, 0))
    vec = pl.BlockSpec((1, d), lambda i: (0, 0))
    return pl.pallas_call(
        _ln_kernel,
        grid=(m // tm,),
        in_specs=[row, vec, vec],
        out_specs=row,
        out_shape=jax.ShapeDtypeStruct((m, d), F32),
        compiler_params=_cparams("parallel"),
        name="ln_out",
    )(pre, g, bvec)


def _ffn2_kernel(alpha, n_k, a_ref, w_ref, r_ref, gate_ref, o_ref):
    kk = pl.program_id(2)
    last = n_k - 1
    part = lambda: jnp.dot(a_ref[...], w_ref[...], preferred_element_type=F32)
    if last == 0:
        o_ref[...] = alpha * r_ref[...] + gate_ref[...] * part()
        return

    @pl.when(kk == 0)
    def _():
        o_ref[...] = part()

    if last > 1:
        @pl.when(jnp.logical_and(kk > 0, kk < last))
        def _():
            o_ref[...] += part()

    @pl.when(kk == last)
    def _():
        o_ref[...] = alpha * r_ref[...] + gate_ref[...] * (o_ref[...] + part())


def _ffn2_resid(a, w, resid, mod3, gate_idx, alpha, t):
    m, f = a.shape
    d = w.shape[1]
    tm, tn, tk = min(MXU_TILE, m, t), min(MXU_TILE, d), min(4 * MXU_TILE, f)
    per_batch = t // tm
    nd = d // tn
    return pl.pallas_call(
        functools.partial(_ffn2_kernel, alpha, f // tk),
        grid=(m // tm, nd, f // tk),
        in_specs=[pl.BlockSpec((tm, tk), lambda i, j, kk: (i, kk)),
                  pl.BlockSpec((tk, tn), lambda i, j, kk: (kk, j)),
                  pl.BlockSpec((tm, tn), lambda i, j, kk: (i, j)),
                  pl.BlockSpec((None, 1, tn), lambda i, j, kk: (i // per_batch, 0, gate_idx * nd + j))],
        out_specs=pl.BlockSpec((tm, tn), lambda i, j, kk: (i, j)),
        out_shape=jax.ShapeDtypeStruct((m, d), F32),
        compiler_params=_cparams("parallel", "parallel", "arbitrary", vmem=VMEM_LIMIT_WIDE),
        name="ffn_down_resid",
    )(a, w, resid, mod3)


def _pos_tables(rows, cols, dim):
    quarter = dim // 4
    omega = 1.0 / (POS_BASE ** (jnp.arange(quarter, dtype=F32) / quarter))

    def emb(p):
        a = p[:, None] * omega[None, :]
        return jnp.concatenate([jnp.sin(a), jnp.cos(a)], axis=-1)

    return (emb(jnp.arange(rows, dtype=F32))[:, None, :], emb(jnp.arange(cols, dtype=F32))[None, :, :])


def kernel(x, c, ctx, c_ctx, w_ada, b_ada, w_in, lb_logits, a_norm_g, w_proj_a, v_norm_g, v_norm_b,
           w_s, b_s, w_proj_b, w_out, ln1_g, ln1_b, w_ff1, w_ff2, ln2_g, ln2_b):
    depth = w_ada.shape[0]
    assert depth == 1, "only the single-layer block is implemented"
    b, t, d = x.shape
    tc = ctx.shape[1]
    a_fdim = lb_logits.shape[-1]
    a_width = w_proj_a.shape[1]
    b_width = w_proj_b.shape[1]
    assert a_fdim == a_width and a_norm_g.shape[-1] == HEAD_DIM
    assert t % GRID_W == 0 and t % SCAN_BLOCK == 0 and tc % SCAN_BLOCK == 0
    assert w_s.shape[2] == SCAN_BLOCK or t % w_s.shape[2] == 0
    alpha = (2.0 * depth) ** 0.25
    m = b * t

    sizes = (a_fdim, a_width, a_fdim, a_fdim, a_width, b_width, b_width, d, d)
    offs = [0]
    for s in sizes:
        offs.append(offs[-1] + s)
    assert offs[-1] == w_in.shape[-1]

    w_in0 = w_in[0]
    seg = lambda first, last: _Cast(w_in0, offs[first], offs[last + 1] - offs[first])
    wq_b = w_in0[:, :offs[1]].astype(BF16)
    ws_b = w_s[0].astype(BF16)
    lb_all = jnp.cumsum(jax.nn.softmax(lb_logits.astype(F32), axis=1), axis=1)
    lb_row = jnp.concatenate([lb_all[0, 0], lb_all[1, 0]])[None, :]
    mix_bias = jnp.repeat(b_s[0].T, b_width // w_s.shape[1], axis=1)
    row_tab, col_tab = _pos_tables(t // GRID_W, GRID_W, d)

    n_rows = -(-(b + 1) // 8) * 8
    cc = jnp.concatenate([c, c_ctx[None, :], jnp.zeros((n_rows - b - 1, d), F32)], axis=0)
    mod = _modulation(cc, w_ada[0], b_ada[0][None, :])
    mod3 = mod.reshape(n_rows, 1, N_MOD * d)

    h = _adaln_pos(x, row_tab, col_tab, mod3, rows_per_step=min(4, t // GRID_W))
    hc = _adaln_ctx(ctx.reshape(b * tc, d), mod3, b, tm=min(256, b * tc))

    scale = HEAD_DIM ** -0.5
    ep_q = functools.partial(_ep_silu, scale)
    ep_g = functools.partial(_ep_silu, 1.0)

    qx, wi_b = _in_proj(h, wq_b, ep_q, [BF16], casts=(seg(1, 1),), name="in_proj_q")
    vx, wf_b = _in_proj(h, wi_b, _ep_cast, [BF16], casts=(seg(2, 3),), name="in_proj_i")
    kkx, lfx, wg_b = _in_proj(h, wf_b, _ep_decay, [BF16, F32], extra=(lb_row,), casts=(seg(4, 4),),
                              name="in_proj_f")
    gx, wuv_b = _in_proj(h, wg_b, ep_g, [BF16], casts=(seg(5, 6),), name="in_proj_g")
    uv, wgate_b = _in_proj(h, wuv_b, _ep_cast, [BF16], casts=(seg(7, 8),), name="in_proj_uv")
    gates, wa_b, wb_b, w1_b = _in_proj(
        h, wgate_b, _ep_sigmoid, [BF16],
        casts=(_Cast(w_proj_a[0], 0, d), _Cast(w_proj_b[0], 0, d), _Cast(w_ff1[0], 0, w_ff1.shape[-1])),
        name="in_proj_gates")

    (qc,) = _in_proj(hc, wq_b, ep_q, [BF16], name="in_proj_q_ctx")
    (vc,) = _in_proj(hc, wi_b, _ep_cast, [BF16], name="in_proj_i_ctx")
    kkc, lfc = _in_proj(hc, wf_b, _ep_decay, [BF16, F32], extra=(lb_row,), name="in_proj_f_ctx")

    ya = _hgrn2_scan((qx, vx, kkx, lfx, gx), (qc, vc, kkc, lfc), a_norm_g[0][None, :], b, t, tc)
    yb = _chunk_mix(uv, v_norm_g[0][None, :], v_norm_b[0][None, :], ws_b, mix_bias,
                    rows_per_step=min(2 * w_s.shape[2], t))

    merged, wo_b = _merge(ya, yb, wa_b, wb_b, gates, casts=(_Cast(w_out[0], 0, d),))
    pre1 = _out_proj_resid(merged, wo_b, x, row_tab, col_tab, mod3, 2, alpha)
    x1, act, w2_b = _ln_ffn_up(pre1, ln1_g[0][None, :], ln1_b[0][None, :], mod3, 3, 4, t, w1_b,
                               casts=(_Cast(w_ff2[0], 0, d),))
    pre2 = _ffn2_resid(act, w2_b, x1, mod3, 5, alpha, t)
    out = _ln(pre2, ln2_g[0][None, :], ln2_b[0][None, :], tm=min(256, m))
    return out.reshape(b, t, d)
```

```python
import functools
import math
from typing import NamedTuple

import jax
import jax.numpy as jnp
from jax import lax
from jax.experimental import pallas as pl
from jax.experimental.pallas import tpu as pltpu

F32 = jnp.float32
BF16 = jnp.bfloat16

GRID_W = 64
LN_EPS = 1e-6
POS_BASE = 10000.0
N_MOD = 6
HEAD_DIM = 128
SCAN_BLOCK = 128
MXU_TILE = 1024
CAST_COLS = 4096
BF16_SUBLANES = 16
LN_ROWS = BF16_SUBLANES
VMEM_LIMIT = 56 * 1024 * 1024
VMEM_LIMIT_WIDE = 60 * 1024 * 1024


def _cparams(*sem, vmem=VMEM_LIMIT):
    return pltpu.CompilerParams(dimension_semantics=sem, vmem_limit_bytes=vmem)


def _sigmoid(x):
    return 0.5 * jnp.tanh(0.5 * x) + 0.5


def _ln_rows(x):
    mu = jnp.mean(x, axis=-1, keepdims=True)
    xc = x - mu
    return xc * lax.rsqrt(jnp.mean(xc * xc, axis=-1, keepdims=True) + LN_EPS)


def _mod_kernel(c_ref, w_ref, b_ref, o_ref):
    a = c_ref[...]
    a = a * _sigmoid(a)
    o_ref[...] = jnp.dot(a.astype(BF16), w_ref[...].astype(BF16),
                         preferred_element_type=F32) + b_ref[...]


def _modulation(cc, w_ada, b_ada):
    rows, d = cc.shape
    n = w_ada.shape[1]
    tn = min(MXU_TILE, n)
    return pl.pallas_call(
        _mod_kernel,
        grid=(n // tn,),
        in_specs=[pl.BlockSpec((rows, d), lambda j: (0, 0)),
                  pl.BlockSpec((d, tn), lambda j: (0, j)),
                  pl.BlockSpec((1, tn), lambda j: (0, j))],
        out_specs=pl.BlockSpec((rows, tn), lambda j: (0, j)),
        out_shape=jax.ShapeDtypeStruct((rows, n), F32),
        compiler_params=_cparams("arbitrary"),
        name="modulation",
    )(cc, w_ada, b_ada)


def _adaln_pos_kernel(x_ref, rt_ref, ct_ref, sh_ref, sc_ref, h_ref):
    half = rt_ref.shape[-1]
    d = 2 * half
    sh = sh_ref[...]
    sc = 1.0 + sc_ref[...]

    def rows_body(r, g):
        rows = pl.ds(pl.multiple_of(g * LN_ROWS, LN_ROWS), LN_ROWS)
        xa = x_ref[r, rows, :half] + rt_ref[r]
        xb = x_ref[r, rows, half:] + ct_ref[0, rows, :]
        mu = (jnp.sum(xa, axis=-1, keepdims=True) + jnp.sum(xb, axis=-1, keepdims=True)) * (1.0 / d)
        xa = xa - mu
        xb = xb - mu
        var = (jnp.sum(xa * xa, axis=-1, keepdims=True) + jnp.sum(xb * xb, axis=-1, keepdims=True)) * (1.0 / d)
        inv = lax.rsqrt(var + LN_EPS)
        h_ref[r, rows, :half] = (xa * inv * sc[:, :half] + sh[:, :half]).astype(BF16)
        h_ref[r, rows, half:] = (xb * inv * sc[:, half:] + sh[:, half:]).astype(BF16)

    groups = x_ref.shape[1] // LN_ROWS
    for r in range(x_ref.shape[0]):
        lax.fori_loop(0, groups, lambda g, c, r=r: (rows_body(r, g), c)[1], 0, unroll=True)


def _adaln_pos(x, row_tab, col_tab, mod3, rows_per_step):
    b, t, d = x.shape
    gr = t // GRID_W
    r = rows_per_step
    x4 = x.reshape(b, gr, GRID_W, d)
    blk = pl.BlockSpec((None, r, GRID_W, d), lambda i, j: (i, j, 0, 0))
    h = pl.pallas_call(
        _adaln_pos_kernel,
        grid=(b, gr // r),
        in_specs=[blk,
                  pl.BlockSpec((r, 1, d // 2), lambda i, j: (j, 0, 0)),
                  pl.BlockSpec((1, GRID_W, d // 2), lambda i, j: (0, 0, 0)),
                  pl.BlockSpec((None, 1, d), lambda i, j: (i, 0, 0)),
                  pl.BlockSpec((None, 1, d), lambda i, j: (i, 0, 1))],
        out_specs=blk,
        out_shape=jax.ShapeDtypeStruct(x4.shape, BF16),
        compiler_params=_cparams("parallel", "parallel"),
        name="adaln_pos",
    )(x4, row_tab, col_tab, mod3, mod3)
    return h.reshape(b * t, d)


def _adaln_rows_kernel(x_ref, sh_ref, sc_ref, h_ref):
    y = _ln_rows(x_ref[...])
    h_ref[...] = (y * (1.0 + sc_ref[...]) + sh_ref[...]).astype(BF16)


def _adaln_ctx(xc, mod3, ctx_row, tm):
    m, d = xc.shape
    return pl.pallas_call(
        _adaln_rows_kernel,
        grid=(m // tm,),
        in_specs=[pl.BlockSpec((tm, d), lambda i: (i, 0)),
                  pl.BlockSpec((None, 1, d), lambda i: (ctx_row, 0, 0)),
                  pl.BlockSpec((None, 1, d), lambda i: (ctx_row, 0, 1))],
        out_specs=pl.BlockSpec((tm, d), lambda i: (i, 0)),
        out_shape=jax.ShapeDtypeStruct((m, d), BF16),
        compiler_params=_cparams("parallel"),
        name="adaln_ctx",
    )(xc, mod3, mod3)


def _ep_silu(scale, acc, o_ref):
    hz = 0.5 * acc
    o_ref[...] = ((hz if scale == 1.0 else hz * scale) * (1.0 + jnp.tanh(hz))).astype(o_ref.dtype)


def _ep_cast(acc, o_ref):
    o_ref[...] = acc.astype(o_ref.dtype)


def _ep_sigmoid(acc, o_ref):
    o_ref[...] = _sigmoid(acc).astype(o_ref.dtype)


def _ep_decay(acc, lb_ref, k_ref, lf_ref):
    lb = lb_ref[...]
    c = 0.5 - 0.5 * lb
    ct = c * jnp.tanh(0.5 * acc)
    k_ref[...] = (c - ct).astype(k_ref.dtype)
    lf_ref[...] = jnp.log2((0.5 + 0.5 * lb) + ct)


class _Cast(NamedTuple):
    src: jax.Array
    col0: int
    width: int


def _cast_blocks(cast, steps):
    cw = math.gcd(cast.col0, cast.width, CAST_COLS)
    ncb = cast.width // cw
    return (cast.src.shape[0] * ncb) // steps, cw, ncb


def _cast_rides(cast, steps):
    rb, _, ncb = _cast_blocks(cast, steps)
    rows = cast.src.shape[0]
    return rb * steps == rows * ncb and rb % BF16_SUBLANES == 0 and rows % rb == 0


def _split_casts(casts, steps):
    return [c for c in casts if _cast_rides(c, steps)]


def _cast_results(casts, steps, ridden):
    ridden = list(ridden)
    return [ridden.pop(0) if _cast_rides(c, steps) else c.src[:, c.col0:c.col0 + c.width].astype(BF16)
            for c in casts]


def _cast_specs(casts, steps, step_of):
    in_specs, out_specs, out_shapes = [], [], []
    for cast in casts:
        src, col0, width = cast
        rows = src.shape[0]
        rb, cw, ncb = _cast_blocks(cast, steps)
        coff = col0 // cw
        in_specs.append(pl.BlockSpec(
            (rb, cw), lambda i, j, ncb=ncb, coff=coff: (step_of(i, j) // ncb, coff + step_of(i, j) % ncb)))
        out_specs.append(pl.BlockSpec(
            (rb, cw), lambda i, j, ncb=ncb: (step_of(i, j) // ncb, step_of(i, j) % ncb)))
        out_shapes.append(jax.ShapeDtypeStruct((rows, width), BF16))
    return in_specs, out_specs, out_shapes


def _do_casts(src_refs, dst_refs):
    for s_ref, d_ref in zip(src_refs, dst_refs, strict=True):
        d_ref[...] = s_ref[...].astype(BF16)


def _proj_kernel(epilogue, n_extra, n_cast, x_ref, w_ref, *refs):
    extras, refs = refs[:n_extra], refs[n_extra:]
    cast_src, refs = refs[:n_cast], refs[n_cast:]
    outs, cast_dst = refs[:len(refs) - n_cast], refs[len(refs) - n_cast:]
    acc = jnp.dot(x_ref[...], w_ref[...], preferred_element_type=F32)
    epilogue(acc, *extras, *outs)
    _do_casts(cast_src, cast_dst)


def _in_proj(h, w, epilogue, out_dtypes, extra=(), casts=(), name="in_proj"):
    m, d = h.shape
    width = w.shape[1]
    tm = min(MXU_TILE, m)
    tn = min(MXU_TILE, width)
    n_i, n_j = m // tm, width // tn
    out_blk = pl.BlockSpec((tm, tn), lambda i, j: (i, j))
    riding = _split_casts(casts, n_i * n_j)
    c_in, c_out, c_shapes = _cast_specs(riding, n_i * n_j, lambda i, j: i * n_j + j)
    res = pl.pallas_call(
        functools.partial(_proj_kernel, epilogue, len(extra), len(riding)),
        grid=(n_i, n_j),
        in_specs=[pl.BlockSpec((tm, d), lambda i, j: (i, 0)),
                  pl.BlockSpec((d, tn), lambda i, j: (0, j))]
                 + [pl.BlockSpec((1, tn), lambda i, j: (0, j)) for _ in extra] + c_in,
        out_specs=[out_blk for _ in out_dtypes] + c_out,
        out_shape=[jax.ShapeDtypeStruct((m, width), dt) for dt in out_dtypes] + c_shapes,
        compiler_params=_cparams("parallel", "arbitrary"),
        name=name,
    )(h, w, *extra, *[c.src for c in riding])
    n_out = len(out_dtypes)
    return list(res[:n_out]) + _cast_results(casts, n_i * n_j, res[n_out:])


def _mid_rows(b, m):
    c, w = b.shape
    assert m in (1, 2, 4)
    if m == 1:
        odd = (lax.broadcasted_iota(jnp.int32, (c, w), 0) & 1) == 1
        return jnp.where(odd, pltpu.roll(b, 1, 0), b)
    b3 = b.reshape(c // 8, 8, w)
    if m == 4:
        return jnp.broadcast_to(b3[:, 3:4, :], b3.shape).reshape(c, w)
    low = lax.broadcasted_iota(jnp.int32, b3.shape, 1) < 4
    return jnp.where(low, jnp.broadcast_to(b3[:, 1:2, :], b3.shape),
                     jnp.broadcast_to(b3[:, 5:6, :], b3.shape)).reshape(c, w)


def _block_sums(tri, lf, lb):
    k = lf.shape[1]
    parts = []
    for a in (lf, lb):
        hi = a.astype(BF16)
        parts += [hi, (a - hi.astype(F32)).astype(BF16)]
    cs = jnp.dot(tri, jnp.concatenate(parts, axis=1), preferred_element_type=F32)
    return jnp.concatenate([cs[:, :k] + cs[:, k:2 * k], cs[:, 2 * k:3 * k] + cs[:, 3 * k:]], axis=1)


def _state_terms(v, kf, kb, bcat, lb):
    c, k = lb.shape
    tot = bcat[c - 1:c]
    ks = jnp.concatenate([kf * jnp.exp2(tot[:, :k] - bcat[:, :k]),
                          kb * jnp.exp2(bcat[:, k:] - lb)], axis=1)
    return pl.dot(v, ks.astype(BF16), trans_a=True), jnp.exp2(tot)


def _level_products(q, kf, kb, bcat, lb, par_ref):
    c, k = lb.shape
    xf, xb = bcat[:, :k], bcat[:, k:] - lb
    prods = []
    m, level = 1, 0
    while m < c:
        if m % 8 == 0:
            split = lambda a: a.reshape(c // (2 * m), 2, m, a.shape[-1])
            join = lambda even, odd: jnp.concatenate([even, odd], axis=1).reshape(c, k)
            xf4, xb4, kf4, kb4 = split(xf), split(xb), split(kf), split(kb)
            mid = split(bcat)[:, 0:1, m - 1:m, :]
            mf, mb = mid[..., :k], mid[..., k:]
            eq = join(mb - xb4[:, 0:1], xf4[:, 1:2] - mf)
            ek = join(mf - xf4[:, 0:1], xb4[:, 1:2] - mb)
            ksel = join(kf4[:, 0:1], kb4[:, 1:2])
        else:
            mid = _mid_rows(bcat, m)
            df = xf - mid[:, :k]
            db = mid[:, k:] - xb
            eq = jnp.minimum(df, db)
            ek = -jnp.maximum(df, db)
            ksel = jnp.where(par_ref[level] > 0, kb, kf)
        prods.append(pl.dot(q * jnp.exp2(eq), ksel * jnp.exp2(ek), trans_b=True))
        m *= 2
        level += 1
    prods.append(pl.dot(q, kf + kb, trans_b=True))
    return prods


def _assemble(prods):
    c = prods[0].shape[0]
    x = (lax.broadcasted_iota(jnp.int32, (c, c), 0) ^ lax.broadcasted_iota(jnp.int32, (c, c), 1))
    att = prods[-2]
    m = c // 2
    for a in reversed(prods[:-2]):
        att = jnp.where(x < m, a, att)
        m //= 2
    return jnp.where(x < 1, prods[-1], att).astype(BF16)


def _entry_factors(q, bcat, lb):
    c, k = lb.shape
    xb = bcat[:, k:] - lb
    return jnp.concatenate([q * jnp.exp2(bcat[:, :k]),
                            q * jnp.exp2(bcat[c - 1:c, k:] - xb)], axis=1).astype(BF16)


def _scan_kernel(nx, nc,
                 qx, vx, kfx, kbx, lfx, lbx, gx, qc, vc, kfc, kbc, lfc, lbc, gain_ref, tri_ref, par_ref,
                 y_ref, oi_ref, qi_ref, u_ref, d_ref, s_ref, b_ref):
    c = SCAN_BLOCK
    k = HEAD_DIM
    f32 = lambda ref, rows: ref[rows, :].astype(F32)

    ctx_rows = [pl.ds(j * c, c) for j in range(nc)]
    ctx_sums = [_block_sums(tri_ref[...], lfc[rows, :], lbc[rows, :]) for rows in ctx_rows]

    group = math.gcd(nx, 4)
    n_groups = nx // group

    def sums(g, slot):
        for i in range(group):
            rows = pl.ds(pl.multiple_of((g * group + i) * c, c), c)
            b_ref[slot, pl.ds(i * c, c), :] = _block_sums(tri_ref[...], lfx[rows, :], lbx[rows, :])

    sums(0, 0)
    for j, (rows, bcat) in enumerate(zip(ctx_rows, ctx_sums)):
        u_t, dec = _state_terms(vc[rows, :], f32(kfc, rows), f32(kbc, rows), bcat, lbc[rows, :])
        u_ref[j] = u_t
        d_ref[j] = dec

    def do_group(p, slot):
        sums(jnp.minimum(p + 1, n_groups - 1), 1 - slot)
        rows = [pl.ds(pl.multiple_of((p * group + i) * c, c), c) for i in range(group)]
        bcats = [b_ref[slot, pl.ds(i * c, c), :] for i in range(group)]
        qkk = [(f32(qx, rw), f32(kfx, rw), f32(kbx, rw)) for rw in rows]
        prods = [_level_products(q, kf, kb, bcat, lbx[rw, :], par_ref)
                 for (q, kf, kb), rw, bcat in zip(qkk, rows, bcats)]
        for i, ((_, kf, kb), rw, bcat) in enumerate(zip(qkk, rows, bcats)):
            u_t, dec = _state_terms(vx[rw, :], kf, kb, bcat, lbx[rw, :])
            u_ref[nc + p * group + i] = u_t
            d_ref[nc + p * group + i] = dec
        outs = [jnp.dot(_assemble(pr), vx[rw, :], preferred_element_type=F32) for pr, rw in zip(prods, rows)]
        for (q, _, _), rw, bcat in zip(qkk, rows, bcats):
            qi_ref[rw, :] = _entry_factors(q, bcat, lbx[rw, :])
        for o, rw in zip(outs, rows):
            oi_ref[rw, :] = o

    def pair_body(pp, carry):
        do_group(2 * pp, 0)
        do_group(2 * pp + 1, 1)
        return carry

    lax.fori_loop(0, n_groups // 2, pair_body, 0)
    if n_groups % 2:
        do_group(n_groups - 1, 0)

    sf = jnp.zeros((k, k), F32)
    sb = jnp.zeros((k, k), F32)
    for j in range(nc):
        sf = d_ref[j][:, :k] * sf + u_ref[j][:, :k]
        jb = nc - 1 - j
        sb = d_ref[jb][:, k:] * sb + u_ref[jb][:, k:]

    def sweep_body(i, carry):
        sf, sb = carry
        s_ref[i, :, :k] = sf.astype(BF16)
        sf = d_ref[nc + i][:, :k] * sf + u_ref[nc + i][:, :k]
        j = nx - 1 - i
        s_ref[j, :, k:] = sb.astype(BF16)
        sb = d_ref[nc + j][:, k:] * sb + u_ref[nc + j][:, k:]
        return sf, sb

    lax.fori_loop(0, nx, sweep_body, (sf, sb))

    gain = gain_ref[...]

    def out_body(n, carry):
        rows = pl.ds(pl.multiple_of(n * c, c), c)
        o = oi_ref[rows, :] + pl.dot(qi_ref[rows, :], s_ref[n], trans_b=True)
        o = o * lax.rsqrt(jnp.mean(o * o, axis=-1, keepdims=True) + LN_EPS) * gain
        y_ref[rows, :] = (o * gx[rows, :].astype(F32)).astype(BF16)
        return carry

    lax.fori_loop(0, nx, out_body, 0, unroll=math.gcd(nx, 16))


def _hgrn2_scan(px, pc, gain, b, t, tc):
    qx, vx, kkx, lfx, gx = px
    qc, vc, kkc, lfc = pc
    k = HEAD_DIM
    heads = qx.shape[1] // k
    nx, nc = t // SCAN_BLOCK, tc // SCAN_BLOCK
    n_levels = SCAN_BLOCK.bit_length() - 1
    bits = (jnp.arange(SCAN_BLOCK)[None, :] >> jnp.arange(n_levels)[:, None]) & 1
    parity = jnp.broadcast_to(bits[:, :, None], (n_levels, SCAN_BLOCK, k)).astype(F32)
    fwd = lambda rows: pl.BlockSpec((rows, k), lambda i, h: (i, h))
    bwd = lambda rows: pl.BlockSpec((rows, k), lambda i, h: (i, h + heads))
    return pl.pallas_call(
        functools.partial(_scan_kernel, nx, nc),
        grid=(b, heads),
        in_specs=[fwd(t), fwd(t), fwd(t), bwd(t), fwd(t), bwd(t), fwd(t),
                  fwd(tc), fwd(tc), fwd(tc), bwd(tc), fwd(tc), bwd(tc),
                  pl.BlockSpec((1, k), lambda i, h: (0, 0)),
                  pl.BlockSpec((SCAN_BLOCK, SCAN_BLOCK), lambda i, h: (0, 0)),
                  pl.BlockSpec(parity.shape, lambda i, h: (0, 0, 0))],
        out_specs=fwd(t),
        out_shape=jax.ShapeDtypeStruct((b * t, heads * k), BF16),
        scratch_shapes=[pltpu.VMEM((t, k), F32),
                        pltpu.VMEM((t, 2 * k), BF16),
                        pltpu.VMEM((nx + nc, k, 2 * k), F32),
                        pltpu.VMEM((nx + nc, 1, 2 * k), F32),
                        pltpu.VMEM((nx, k, 2 * k), BF16),
                        pltpu.VMEM((2, math.gcd(nx, 4) * SCAN_BLOCK, 2 * k), F32)],
        compiler_params=_cparams("parallel", "parallel"),
        name="hgrn2_scan",
    )(qx, vx, kkx, kkx, lfx, lfx, gx, qc, vc, kkc, kkc, lfc, lfc, gain,
      jnp.tril(jnp.ones((SCAN_BLOCK, SCAN_BLOCK), BF16)), parity)


def _mix_kernel(n_groups, chunk, u_ref, v_ref, g_ref, b_ref, ws_ref, bias_ref, y_ref, vn_ref):
    gain, bias = g_ref[...], b_ref[...]
    for r0 in range(0, v_ref.shape[0], LN_ROWS):
        rows = pl.ds(r0, LN_ROWS)
        vn_ref[rows, :] = (_ln_rows(v_ref[rows, :].astype(F32)) * gain + bias).astype(BF16)
    gc = v_ref.shape[1] // n_groups
    for cidx in range(v_ref.shape[0] // chunk):
        rows = pl.ds(cidx * chunk, chunk)
        for g in range(n_groups):
            cols = pl.ds(g * gc, gc)
            mixed = jnp.dot(ws_ref[g], vn_ref[rows, cols], preferred_element_type=F32)
            y_ref[rows, cols] = (u_ref[rows, cols].astype(F32) * (mixed + bias_ref[:, cols])).astype(BF16)


def _chunk_mix(uv, v_g, v_b, w_s, bias, rows_per_step):
    m = uv.shape[0]
    bw = uv.shape[1] // 2
    groups, chunk, _ = w_s.shape
    tm = rows_per_step
    return pl.pallas_call(
        functools.partial(_mix_kernel, groups, chunk),
        grid=(m // tm,),
        in_specs=[pl.BlockSpec((tm, bw), lambda i: (i, 0)),
                  pl.BlockSpec((tm, bw), lambda i: (i, 1)),
                  pl.BlockSpec((1, bw), lambda i: (0, 0)),
                  pl.BlockSpec((1, bw), lambda i: (0, 0)),
                  pl.BlockSpec((groups, chunk, chunk), lambda i: (0, 0, 0)),
                  pl.BlockSpec((chunk, bw), lambda i: (0, 0))],
        out_specs=pl.BlockSpec((tm, bw), lambda i: (i, 0)),
        out_shape=jax.ShapeDtypeStruct((m, bw), BF16),
        scratch_shapes=[pltpu.VMEM((tm, bw), BF16)],
        compiler_params=_cparams("parallel"),
        name="chunk_mix",
    )(uv, uv, v_g, v_b, w_s, bias)


def _merge_kernel(n_cast, ya_ref, yb_ref, wa_ref, wb_ref, ga_ref, gb_ref, *refs):
    cast_src, o_ref, cast_dst = refs[:n_cast], refs[n_cast], refs[n_cast + 1:]
    a = jnp.dot(ya_ref[...], wa_ref[...], preferred_element_type=F32)
    bq = jnp.dot(yb_ref[...], wb_ref[...], preferred_element_type=F32)
    o_ref[...] = (ga_ref[...].astype(F32) * a + gb_ref[...].astype(F32) * bq).astype(BF16)
    _do_casts(cast_src, cast_dst)


def _merge(ya, yb, wa, wb, gates, casts=()):
    m, ka = ya.shape
    kb = yb.shape[1]
    d = wa.shape[1]
    tm, tn = min(MXU_TILE, m), min(MXU_TILE, d)
    nd = d // tn
    steps = (m // tm) * nd
    riding = _split_casts(casts, steps)
    c_in, c_out, c_shapes = _cast_specs(riding, steps, lambda i, j: i * nd + j)
    res = pl.pallas_call(
        functools.partial(_merge_kernel, len(riding)),
        grid=(m // tm, nd),
        in_specs=[pl.BlockSpec((tm, ka), lambda i, j: (i, 0)),
                  pl.BlockSpec((tm, kb), lambda i, j: (i, 0)),
                  pl.BlockSpec((ka, tn), lambda i, j: (0, j)),
                  pl.BlockSpec((kb, tn), lambda i, j: (0, j)),
                  pl.BlockSpec((tm, tn), lambda i, j: (i, j)),
                  pl.BlockSpec((tm, tn), lambda i, j: (i, j + nd))] + c_in,
        out_specs=[pl.BlockSpec((tm, tn), lambda i, j: (i, j))] + c_out,
        out_shape=[jax.ShapeDtypeStruct((m, d), BF16)] + c_shapes,
        compiler_params=_cparams("parallel", "arbitrary"),
        name="branch_merge",
    )(ya, yb, wa, wb, gates, gates, *[c.src for c in riding])
    return [res[0]] + _cast_results(casts, steps, res[1:])


def _resid_pos_kernel(alpha, n_row_code, x_ref, w_ref, r_ref, rt_ref, ct_ref, gate_ref, o_ref):
    y = jnp.dot(x_ref[...], w_ref[...], preferred_element_type=F32)
    is_row = (pl.program_id(1) < n_row_code).astype(F32)
    xp = r_ref[...] + is_row * rt_ref[...] + (1.0 - is_row) * ct_ref[...]
    o_ref[...] = alpha * xp.reshape(o_ref.shape) + gate_ref[...] * y


def _out_proj_resid(xin, w, x, row_tab, col_tab, mod3, gate_idx, alpha):
    m, kdim = xin.shape
    b, t, d = x.shape
    half = d // 2
    tm, tn = min(MXU_TILE, m, t), min(MXU_TILE, half)
    assert tm % GRID_W == 0 and half % tn == 0
    per_batch = t // tm
    nd, nh = d // tn, half // tn
    r = tm // GRID_W
    x3 = x.reshape(m // GRID_W, GRID_W, d)
    return pl.pallas_call(
        functools.partial(_resid_pos_kernel, alpha, nh),
        grid=(m // tm, nd),
        in_specs=[pl.BlockSpec((tm, kdim), lambda i, j: (i, 0)),
                  pl.BlockSpec((kdim, tn), lambda i, j: (0, j)),
                  pl.BlockSpec((r, GRID_W, tn), lambda i, j: (i, 0, j)),
                  pl.BlockSpec((r, 1, tn), lambda i, j: (i % per_batch, 0, jnp.minimum(j, nh - 1))),
                  pl.BlockSpec((1, GRID_W, tn), lambda i, j: (0, 0, jnp.maximum(j - nh, 0))),
                  pl.BlockSpec((None, 1, tn), lambda i, j: (i // per_batch, 0, gate_idx * nd + j))],
        out_specs=pl.BlockSpec((tm, tn), lambda i, j: (i, j)),
        out_shape=jax.ShapeDtypeStruct((m, d), F32),
        compiler_params=_cparams("parallel", "arbitrary", vmem=VMEM_LIMIT_WIDE),
        name="out_proj_resid",
    )(xin, w, x3, row_tab, col_tab, mod3)


def _ln_ffn_up_kernel(n_cast, p_ref, g_ref, b_ref, sh_ref, sc_ref, w_ref, *refs):
    cast_src, refs = refs[:n_cast], refs[n_cast:]
    x1_ref, act_ref = refs[0], refs[1]
    cast_dst, h_ref = refs[2:2 + n_cast], refs[2 + n_cast]
    i, j = pl.program_id(0), pl.program_id(1)
    rows = p_ref.shape[0]

    def prepare():
        g, b, sc, sh = g_ref[...], b_ref[...], 1.0 + sc_ref[...], sh_ref[...]
        for r0 in range(0, rows, LN_ROWS):
            x = _ln_rows(p_ref[pl.ds(r0, LN_ROWS), :]) * g + b
            x1_ref[pl.ds(r0, LN_ROWS), :] = x
            h = _ln_rows(x) * sc + sh
            h_ref[i % 2, pl.ds(pl.multiple_of(j * rows + r0, LN_ROWS), LN_ROWS), :] = h.astype(BF16)
        _do_casts(cast_src, cast_dst)

    @pl.when(i == 0)
    def _():
        prepare()

    @pl.when(i > 0)
    def _():
        y = jnp.dot(h_ref[(i - 1) % 2], w_ref[...], preferred_element_type=F32)
        act_ref[...] = jnp.square(jnp.maximum(y, 0.0)).astype(BF16)
        prepare()


def _ln_ffn_up(pre, g, bvec, mod3, shift_idx, scale_idx, t, w, casts=()):
    m, d = pre.shape
    n = w.shape[1]
    tm, tn = min(MXU_TILE, m, t), min(MXU_TILE, n)
    n_i, n_j = m // tm, n // tn
    rows = tm // n_j
    assert rows * n_j == tm and rows % BF16_SUBLANES == 0
    per_batch = t // tm
    tile = lambda i: jnp.minimum(i, n_i - 1)
    prev = lambda i: jnp.maximum(i - 1, 0)
    steps = n_i * n_j
    slab = pl.BlockSpec((rows, d), lambda i, j: (jnp.minimum(i * n_j + j, steps - 1), 0))
    vec = pl.BlockSpec((1, d), lambda i, j: (0, 0))
    riding = _split_casts(casts, steps)
    c_in, c_out, c_shapes = _cast_specs(riding, steps, lambda i, j: jnp.where(i > 0, (i - 1) * n_j + j, 0))
    res = pl.pallas_call(
        functools.partial(_ln_ffn_up_kernel, len(riding)),
        grid=(n_i + 1, n_j),
        in_specs=[slab, vec, vec,
                  pl.BlockSpec((None, 1, d), lambda i, j: (tile(i) // per_batch, 0, shift_idx)),
                  pl.BlockSpec((None, 1, d), lambda i, j: (tile(i) // per_batch, 0, scale_idx)),
                  pl.BlockSpec((d, tn), lambda i, j: (0, jnp.where(i > 0, j, 0)))] + c_in,
        out_specs=[slab, pl.BlockSpec((tm, tn), lambda i, j: (prev(i), jnp.where(i > 0, j, 0)))] + c_out,
        out_shape=[jax.ShapeDtypeStruct((m, d), F32), jax.ShapeDtypeStruct((m, n), BF16)] + c_shapes,
        scratch_shapes=[pltpu.VMEM((2, tm, d), BF16)],
        compiler_params=_cparams("arbitrary", "arbitrary"),
        name="ln_ffn_up",
    )(pre, g, bvec, mod3, mod3, w, *[c.src for c in riding])
    return [res[0], res[1]] + _cast_results(casts, steps, res[2:])


def _ln_kernel(p_ref, g_ref, b_ref, o_ref):
    g, b = g_ref[...], b_ref[...]

    def rows_body(i, c):
        rows = pl.ds(pl.multiple_of(i * LN_ROWS, LN_ROWS), LN_ROWS)
        o_ref[rows, :] = _ln_rows(p_ref[rows, :]) * g + b
        return c

    lax.fori_loop(0, p_ref.shape[0] // LN_ROWS, rows_body, 0, unroll=True)


def _ln(pre, g, bvec, tm):
    m, d = pre.shape
    row = pl.BlockSpec((tm, d), lambda i: (i, 0))
    vec = pl.BlockSpec((1, d), lambda i: (0, 0))
    return pl.pallas_call(
        _ln_kernel,
        grid=(m // tm,),
        in_specs=[row, vec, vec],
        out_specs=row,
        out_shape=jax.ShapeDtypeStruct((m, d), F32),
        compiler_params=_cparams("parallel"),
        name="ln_out",
    )(pre, g, bvec)


def _ffn2_kernel(alpha, n_k, a_ref, w_ref, r_ref, gate_ref, o_ref):
    kk = pl.program_id(2)
    last = n_k - 1
    part = lambda: jnp.dot(a_ref[...], w_ref[...], preferred_element_type=F32)
    if last == 0:
        o_ref[...] = alpha * r_ref[...] + gate_ref[...] * part()
        return

    @pl.when(kk == 0)
    def _():
        o_ref[...] = part()

    if last > 1:
        @pl.when(jnp.logical_and(kk > 0, kk < last))
        def _():
            o_ref[...] += part()

    @pl.when(kk == last)
    def _():
        o_ref[...] = alpha * r_ref[...] + gate_ref[...] * (o_ref[...] + part())


def _ffn2_resid(a, w, resid, mod3, gate_idx, alpha, t):
    m, f = a.shape
    d = w.shape[1]
    tm, tn, tk = min(MXU_TILE, m, t), min(MXU_TILE, d), min(4 * MXU_TILE, f)
    per_batch = t // tm
    nd = d // tn
    return pl.pallas_call(
        functools.partial(_ffn2_kernel, alpha, f // tk),
        grid=(m // tm, nd, f // tk),
        in_specs=[pl.BlockSpec((tm, tk), lambda i, j, kk: (i, kk)),
                  pl.BlockSpec((tk, tn), lambda i, j, kk: (kk, j)),
                  pl.BlockSpec((tm, tn), lambda i, j, kk: (i, j)),
                  pl.BlockSpec((None, 1, tn), lambda i, j, kk: (i // per_batch, 0, gate_idx * nd + j))],
        out_specs=pl.BlockSpec((tm, tn), lambda i, j, kk: (i, j)),
        out_shape=jax.ShapeDtypeStruct((m, d), F32),
        compiler_params=_cparams("parallel", "parallel", "arbitrary", vmem=VMEM_LIMIT_WIDE),
        name="ffn_down_resid",
    )(a, w, resid, mod3)


def _pos_tables(rows, cols, dim):
    quarter = dim // 4
    omega = 1.0 / (POS_BASE ** (jnp.arange(quarter, dtype=F32) / quarter))

    def emb(p):
        a = p[:, None] * omega[None, :]
        return jnp.concatenate([jnp.sin(a), jnp.cos(a)], axis=-1)

    return (emb(jnp.arange(rows, dtype=F32))[:, None, :], emb(jnp.arange(cols, dtype=F32))[None, :, :])


def kernel(x, c, ctx, c_ctx, w_ada, b_ada, w_in, lb_logits, a_norm_g, w_proj_a, v_norm_g, v_norm_b,
           w_s, b_s, w_proj_b, w_out, ln1_g, ln1_b, w_ff1, w_ff2, ln2_g, ln2_b):
    depth = w_ada.shape[0]
    assert depth == 1, "only the single-layer block is implemented"
    b, t, d = x.shape
    tc = ctx.shape[1]
    a_fdim = lb_logits.shape[-1]
    a_width = w_proj_a.shape[1]
    b_width = w_proj_b.shape[1]
    assert a_fdim == a_width and a_norm_g.shape[-1] == HEAD_DIM
    assert t % GRID_W == 0 and t % SCAN_BLOCK == 0 and tc % SCAN_BLOCK == 0
    assert w_s.shape[2] == SCAN_BLOCK or t % w_s.shape[2] == 0
    alpha = (2.0 * depth) ** 0.25
    m = b * t

    sizes = (a_fdim, a_width, a_fdim, a_fdim, a_width, b_width, b_width, d, d)
    offs = [0]
    for s in sizes:
        offs.append(offs[-1] + s)
    assert offs[-1] == w_in.shape[-1]

    w_in0 = w_in[0]
    seg = lambda first, last: _Cast(w_in0, offs[first], offs[last + 1] - offs[first])
    wq_b = w_in0[:, :offs[1]].astype(BF16)
    ws_b = w_s[0].astype(BF16)
    lb_all = jnp.cumsum(jax.nn.softmax(lb_logits.astype(F32), axis=1), axis=1)
    lb_row = jnp.concatenate([lb_all[0, 0], lb_all[1, 0]])[None, :]
    mix_bias = jnp.repeat(b_s[0].T, b_width // w_s.shape[1], axis=1)
    row_tab, col_tab = _pos_tables(t // GRID_W, GRID_W, d)

    n_rows = -(-(b + 1) // 8) * 8
    cc = jnp.concatenate([c, c_ctx[None, :], jnp.zeros((n_rows - b - 1, d), F32)], axis=0)
    mod = _modulation(cc, w_ada[0], b_ada[0][None, :])
    mod3 = mod.reshape(n_rows, 1, N_MOD * d)

    h = _adaln_pos(x, row_tab, col_tab, mod3, rows_per_step=min(8, t // GRID_W))
    hc = _adaln_ctx(ctx.reshape(b * tc, d), mod3, b, tm=min(256, b * tc))

    scale = HEAD_DIM ** -0.5
    ep_q = functools.partial(_ep_silu, scale)
    ep_g = functools.partial(_ep_silu, 1.0)

    qx, wi_b = _in_proj(h, wq_b, ep_q, [BF16], casts=(seg(1, 1),), name="in_proj_q")
    vx, wf_b = _in_proj(h, wi_b, _ep_cast, [BF16], casts=(seg(2, 3),), name="in_proj_i")
    kkx, lfx, wg_b = _in_proj(h, wf_b, _ep_decay, [BF16, F32], extra=(lb_row,), casts=(seg(4, 4),),
                              name="in_proj_f")
    gx, wuv_b = _in_proj(h, wg_b, ep_g, [BF16], casts=(seg(5, 6),), name="in_proj_g")
    uv, wgate_b = _in_proj(h, wuv_b, _ep_cast, [BF16], casts=(seg(7, 8),), name="in_proj_uv")
    gates, wa_b, wb_b, w1_b = _in_proj(
        h, wgate_b, _ep_sigmoid, [BF16],
        casts=(_Cast(w_proj_a[0], 0, d), _Cast(w_proj_b[0], 0, d), _Cast(w_ff1[0], 0, w_ff1.shape[-1])),
        name="in_proj_gates")

    (qc,) = _in_proj(hc, wq_b, ep_q, [BF16], name="in_proj_q_ctx")
    (vc,) = _in_proj(hc, wi_b, _ep_cast, [BF16], name="in_proj_i_ctx")
    kkc, lfc = _in_proj(hc, wf_b, _ep_decay, [BF16, F32], extra=(lb_row,), name="in_proj_f_ctx")

    ya = _hgrn2_scan((qx, vx, kkx, lfx, gx), (qc, vc, kkc, lfc), a_norm_g[0][None, :], b, t, tc)
    yb = _chunk_mix(uv, v_norm_g[0][None, :], v_norm_b[0][None, :], ws_b, mix_bias,
                    rows_per_step=min(4 * w_s.shape[2], t))

    merged, wo_b = _merge(ya, yb, wa_b, wb_b, gates, casts=(_Cast(w_out[0], 0, d),))
    pre1 = _out_proj_resid(merged, wo_b, x, row_tab, col_tab, mod3, 2, alpha)
    x1, act, w2_b = _ln_ffn_up(pre1, ln1_g[0][None, :], ln1_b[0][None, :], mod3, 3, 4, t, w1_b,
                               casts=(_Cast(w_ff2[0], 0, d),))
    pre2 = _ffn2_resid(act, w2_b, x1, mod3, 5, alpha, t)
    out = _ln(pre2, ln2_g[0][None, :], ln2_b[0][None, :], tm=min(512, m))
    return out.reshape(b, t, d)
```

```python
import functools
import math
from typing import NamedTuple

import jax
import jax.numpy as jnp
from jax import lax
from jax.experimental import pallas as pl
from jax.experimental.pallas import tpu as pltpu

F32 = jnp.float32
BF16 = jnp.bfloat16

GRID_W = 64
LN_EPS = 1e-6
POS_BASE = 10000.0
N_MOD = 6
HEAD_DIM = 128
SCAN_BLOCK = 128
MXU_TILE = 1024
CAST_COLS = 4096
BF16_SUBLANES = 16
LN_ROWS = BF16_SUBLANES
VMEM_LIMIT = 56 * 1024 * 1024
VMEM_LIMIT_WIDE = 60 * 1024 * 1024


def _cparams(*sem, vmem=VMEM_LIMIT):
    return pltpu.CompilerParams(dimension_semantics=sem, vmem_limit_bytes=vmem)


def _sigmoid(x):
    return 0.5 * jnp.tanh(0.5 * x) + 0.5


def _ln_rows(x):
    mu = jnp.mean(x, axis=-1, keepdims=True)
    xc = x - mu
    return xc * lax.rsqrt(jnp.mean(xc * xc, axis=-1, keepdims=True) + LN_EPS)


def _mod_kernel(c_ref, w_ref, b_ref, o_ref):
    a = c_ref[...]
    a = a * _sigmoid(a)
    o_ref[...] = jnp.dot(a.astype(BF16), w_ref[...].astype(BF16),
                         preferred_element_type=F32) + b_ref[...]


def _modulation(cc, w_ada, b_ada):
    rows, d = cc.shape
    n = w_ada.shape[1]
    tn = min(MXU_TILE, n)
    return pl.pallas_call(
        _mod_kernel,
        grid=(n // tn,),
        in_specs=[pl.BlockSpec((rows, d), lambda j: (0, 0)),
                  pl.BlockSpec((d, tn), lambda j: (0, j)),
                  pl.BlockSpec((1, tn), lambda j: (0, j))],
        out_specs=pl.BlockSpec((rows, tn), lambda j: (0, j)),
        out_shape=jax.ShapeDtypeStruct((rows, n), F32),
        compiler_params=_cparams("arbitrary"),
        name="modulation",
    )(cc, w_ada, b_ada)


def _adaln_pos_kernel(x_ref, rt_ref, ct_ref, sh_ref, sc_ref, h_ref):
    half = rt_ref.shape[-1]
    d = 2 * half
    sh = sh_ref[...]
    sc = 1.0 + sc_ref[...]

    def rows_body(r, g):
        rows = pl.ds(pl.multiple_of(g * LN_ROWS, LN_ROWS), LN_ROWS)
        xa = x_ref[r, rows, :half] + rt_ref[r]
        xb = x_ref[r, rows, half:] + ct_ref[0, rows, :]
        mu = (jnp.sum(xa, axis=-1, keepdims=True) + jnp.sum(xb, axis=-1, keepdims=True)) * (1.0 / d)
        xa = xa - mu
        xb = xb - mu
        var = (jnp.sum(xa * xa, axis=-1, keepdims=True) + jnp.sum(xb * xb, axis=-1, keepdims=True)) * (1.0 / d)
        inv = lax.rsqrt(var + LN_EPS)
        h_ref[r, rows, :half] = (xa * inv * sc[:, :half] + sh[:, :half]).astype(BF16)
        h_ref[r, rows, half:] = (xb * inv * sc[:, half:] + sh[:, half:]).astype(BF16)

    groups = x_ref.shape[1] // LN_ROWS
    for r in range(x_ref.shape[0]):
        lax.fori_loop(0, groups, lambda g, c, r=r: (rows_body(r, g), c)[1], 0, unroll=True)


def _adaln_pos(x, row_tab, col_tab, mod3, rows_per_step):
    b, t, d = x.shape
    gr = t // GRID_W
    r = rows_per_step
    x4 = x.reshape(b, gr, GRID_W, d)
    blk = pl.BlockSpec((None, r, GRID_W, d), lambda i, j: (i, j, 0, 0))
    h = pl.pallas_call(
        _adaln_pos_kernel,
        grid=(b, gr // r),
        in_specs=[blk,
                  pl.BlockSpec((r, 1, d // 2), lambda i, j: (j, 0, 0)),
                  pl.BlockSpec((1, GRID_W, d // 2), lambda i, j: (0, 0, 0)),
                  pl.BlockSpec((None, 1, d), lambda i, j: (i, 0, 0)),
                  pl.BlockSpec((None, 1, d), lambda i, j: (i, 0, 1))],
        out_specs=blk,
        out_shape=jax.ShapeDtypeStruct(x4.shape, BF16),
        compiler_params=_cparams("parallel", "parallel"),
        name="adaln_pos",
    )(x4, row_tab, col_tab, mod3, mod3)
    return h.reshape(b * t, d)


def _adaln_rows_kernel(x_ref, sh_ref, sc_ref, h_ref):
    y = _ln_rows(x_ref[...])
    h_ref[...] = (y * (1.0 + sc_ref[...]) + sh_ref[...]).astype(BF16)


def _adaln_ctx(xc, mod3, ctx_row, tm):
    m, d = xc.shape
    return pl.pallas_call(
        _adaln_rows_kernel,
        grid=(m // tm,),
        in_specs=[pl.BlockSpec((tm, d), lambda i: (i, 0)),
                  pl.BlockSpec((None, 1, d), lambda i: (ctx_row, 0, 0)),
                  pl.BlockSpec((None, 1, d), lambda i: (ctx_row, 0, 1))],
        out_specs=pl.BlockSpec((tm, d), lambda i: (i, 0)),
        out_shape=jax.ShapeDtypeStruct((m, d), BF16),
        compiler_params=_cparams("parallel"),
        name="adaln_ctx",
    )(xc, mod3, mod3)


def _ep_silu(scale, acc, o_ref):
    hz = 0.5 * acc
    o_ref[...] = ((hz if scale == 1.0 else hz * scale) * (1.0 + jnp.tanh(hz))).astype(o_ref.dtype)


def _ep_cast(acc, o_ref):
    o_ref[...] = acc.astype(o_ref.dtype)


def _ep_sigmoid(acc, o_ref):
    o_ref[...] = _sigmoid(acc).astype(o_ref.dtype)


def _ep_decay(acc, lb_ref, k_ref, lf_ref):
    lb = lb_ref[...]
    c = 0.5 - 0.5 * lb
    ct = c * jnp.tanh(0.5 * acc)
    k_ref[...] = (c - ct).astype(k_ref.dtype)
    lf_ref[...] = jnp.log2((0.5 + 0.5 * lb) + ct)


class _Cast(NamedTuple):
    src: jax.Array
    col0: int
    width: int


def _cast_blocks(cast, steps):
    cw = math.gcd(cast.col0, cast.width, CAST_COLS)
    ncb = cast.width // cw
    return (cast.src.shape[0] * ncb) // steps, cw, ncb


def _cast_rides(cast, steps):
    rb, _, ncb = _cast_blocks(cast, steps)
    rows = cast.src.shape[0]
    return rb * steps == rows * ncb and rb % BF16_SUBLANES == 0 and rows % rb == 0


def _split_casts(casts, steps):
    return [c for c in casts if _cast_rides(c, steps)]


def _cast_results(casts, steps, ridden):
    ridden = list(ridden)
    return [ridden.pop(0) if _cast_rides(c, steps) else c.src[:, c.col0:c.col0 + c.width].astype(BF16)
            for c in casts]


def _cast_specs(casts, steps, step_of):
    in_specs, out_specs, out_shapes = [], [], []
    for cast in casts:
        src, col0, width = cast
        rows = src.shape[0]
        rb, cw, ncb = _cast_blocks(cast, steps)
        coff = col0 // cw
        in_specs.append(pl.BlockSpec(
            (rb, cw), lambda i, j, ncb=ncb, coff=coff: (step_of(i, j) // ncb, coff + step_of(i, j) % ncb)))
        out_specs.append(pl.BlockSpec(
            (rb, cw), lambda i, j, ncb=ncb: (step_of(i, j) // ncb, step_of(i, j) % ncb)))
        out_shapes.append(jax.ShapeDtypeStruct((rows, width), BF16))
    return in_specs, out_specs, out_shapes


def _do_casts(src_refs, dst_refs):
    for s_ref, d_ref in zip(src_refs, dst_refs, strict=True):
        d_ref[...] = s_ref[...].astype(BF16)


def _proj_kernel(epilogue, n_extra, n_cast, x_ref, w_ref, *refs):
    extras, refs = refs[:n_extra], refs[n_extra:]
    cast_src, refs = refs[:n_cast], refs[n_cast:]
    outs, cast_dst = refs[:len(refs) - n_cast], refs[len(refs) - n_cast:]
    acc = jnp.dot(x_ref[...], w_ref[...], preferred_element_type=F32)
    epilogue(acc, *extras, *outs)
    _do_casts(cast_src, cast_dst)


def _in_proj(h, w, epilogue, out_dtypes, extra=(), casts=(), name="in_proj"):
    m, d = h.shape
    width = w.shape[1]
    tm = min(MXU_TILE, m)
    tn = min(MXU_TILE, width)
    n_i, n_j = m // tm, width // tn
    out_blk = pl.BlockSpec((tm, tn), lambda i, j: (i, j))
    riding = _split_casts(casts, n_i * n_j)
    c_in, c_out, c_shapes = _cast_specs(riding, n_i * n_j, lambda i, j: i * n_j + j)
    res = pl.pallas_call(
        functools.partial(_proj_kernel, epilogue, len(extra), len(riding)),
        grid=(n_i, n_j),
        in_specs=[pl.BlockSpec((tm, d), lambda i, j: (i, 0)),
                  pl.BlockSpec((d, tn), lambda i, j: (0, j))]
                 + [pl.BlockSpec((1, tn), lambda i, j: (0, j)) for _ in extra] + c_in,
        out_specs=[out_blk for _ in out_dtypes] + c_out,
        out_shape=[jax.ShapeDtypeStruct((m, width), dt) for dt in out_dtypes] + c_shapes,
        compiler_params=_cparams("parallel", "arbitrary"),
        name=name,
    )(h, w, *extra, *[c.src for c in riding])
    n_out = len(out_dtypes)
    return list(res[:n_out]) + _cast_results(casts, n_i * n_j, res[n_out:])


def _mid_rows(b, m):
    c, w = b.shape
    assert m in (1, 2, 4)
    if m == 1:
        odd = (lax.broadcasted_iota(jnp.int32, (c, w), 0) & 1) == 1
        return jnp.where(odd, pltpu.roll(b, 1, 0), b)
    b3 = b.reshape(c // 8, 8, w)
    if m == 4:
        return jnp.broadcast_to(b3[:, 3:4, :], b3.shape).reshape(c, w)
    low = lax.broadcasted_iota(jnp.int32, b3.shape, 1) < 4
    return jnp.where(low, jnp.broadcast_to(b3[:, 1:2, :], b3.shape),
                     jnp.broadcast_to(b3[:, 5:6, :], b3.shape)).reshape(c, w)


def _block_sums(tri, lf, lb):
    k = lf.shape[1]
    parts = []
    for a in (lf, lb):
        hi = a.astype(BF16)
        parts += [hi, (a - hi.astype(F32)).astype(BF16)]
    cs = jnp.dot(tri, jnp.concatenate(parts, axis=1), preferred_element_type=F32)
    return jnp.concatenate([cs[:, :k] + cs[:, k:2 * k], cs[:, 2 * k:3 * k] + cs[:, 3 * k:]], axis=1)


def _state_terms(v, kf, kb, bcat, lb):
    c, k = lb.shape
    tot = bcat[c - 1:c]
    ks = jnp.concatenate([kf * jnp.exp2(tot[:, :k] - bcat[:, :k]),
                          kb * jnp.exp2(bcat[:, k:] - lb)], axis=1)
    return pl.dot(v, ks.astype(BF16), trans_a=True), jnp.exp2(tot)


def _level_products(q, kf, kb, bcat, lb, par_ref):
    c, k = lb.shape
    xf, xb = bcat[:, :k], bcat[:, k:] - lb
    prods = []
    m, level = 1, 0
    while m < c:
        if m % 8 == 0:
            split = lambda a: a.reshape(c // (2 * m), 2, m, a.shape[-1])
            join = lambda even, odd: jnp.concatenate([even, odd], axis=1).reshape(c, k)
            xf4, xb4, kf4, kb4 = split(xf), split(xb), split(kf), split(kb)
            mid = split(bcat)[:, 0:1, m - 1:m, :]
            mf, mb = mid[..., :k], mid[..., k:]
            eq = join(mb - xb4[:, 0:1], xf4[:, 1:2] - mf)
            ek = join(mf - xf4[:, 0:1], xb4[:, 1:2] - mb)
            ksel = join(kf4[:, 0:1], kb4[:, 1:2])
        else:
            mid = _mid_rows(bcat, m)
            df = xf - mid[:, :k]
            db = mid[:, k:] - xb
            eq = jnp.minimum(df, db)
            ek = -jnp.maximum(df, db)
            ksel = jnp.where(par_ref[level] > 0, kb, kf)
        prods.append(pl.dot(q * jnp.exp2(eq), ksel * jnp.exp2(ek), trans_b=True))
        m *= 2
        level += 1
    prods.append(pl.dot(q, kf + kb, trans_b=True))
    return prods


def _assemble(prods):
    c = prods[0].shape[0]
    x = (lax.broadcasted_iota(jnp.int32, (c, c), 0) ^ lax.broadcasted_iota(jnp.int32, (c, c), 1))
    att = prods[-2]
    m = c // 2
    for a in reversed(prods[:-2]):
        att = jnp.where(x < m, a, att)
        m //= 2
    return jnp.where(x < 1, prods[-1], att).astype(BF16)


def _entry_factors(q, bcat, lb):
    c, k = lb.shape
    xb = bcat[:, k:] - lb
    return jnp.concatenate([q * jnp.exp2(bcat[:, :k]),
                            q * jnp.exp2(bcat[c - 1:c, k:] - xb)], axis=1).astype(BF16)


def _scan_kernel(nx, nc,
                 qx, vx, kfx, kbx, lfx, lbx, gx, qc, vc, kfc, kbc, lfc, lbc, gain_ref, tri_ref, par_ref,
                 y_ref, oi_ref, qi_ref, u_ref, d_ref, s_ref, b_ref):
    c = SCAN_BLOCK
    k = HEAD_DIM
    f32 = lambda ref, rows: ref[rows, :].astype(F32)

    ctx_rows = [pl.ds(j * c, c) for j in range(nc)]
    ctx_sums = [_block_sums(tri_ref[...], lfc[rows, :], lbc[rows, :]) for rows in ctx_rows]

    group = math.gcd(nx, 4)
    n_groups = nx // group

    def sums(g, slot):
        for i in range(group):
            rows = pl.ds(pl.multiple_of((g * group + i) * c, c), c)
            b_ref[slot, pl.ds(i * c, c), :] = _block_sums(tri_ref[...], lfx[rows, :], lbx[rows, :])

    sums(0, 0)
    for j, (rows, bcat) in enumerate(zip(ctx_rows, ctx_sums)):
        u_t, dec = _state_terms(vc[rows, :], f32(kfc, rows), f32(kbc, rows), bcat, lbc[rows, :])
        u_ref[j] = u_t
        d_ref[j] = dec

    def do_group(p, slot):
        sums(jnp.minimum(p + 1, n_groups - 1), 1 - slot)
        rows = [pl.ds(pl.multiple_of((p * group + i) * c, c), c) for i in range(group)]
        bcats = [b_ref[slot, pl.ds(i * c, c), :] for i in range(group)]
        qkk = [(f32(qx, rw), f32(kfx, rw), f32(kbx, rw)) for rw in rows]
        prods = [_level_products(q, kf, kb, bcat, lbx[rw, :], par_ref)
                 for (q, kf, kb), rw, bcat in zip(qkk, rows, bcats)]
        for i, ((_, kf, kb), rw, bcat) in enumerate(zip(qkk, rows, bcats)):
            u_t, dec = _state_terms(vx[rw, :], kf, kb, bcat, lbx[rw, :])
            u_ref[nc + p * group + i] = u_t
            d_ref[nc + p * group + i] = dec
        outs = [jnp.dot(_assemble(pr), vx[rw, :], preferred_element_type=F32) for pr, rw in zip(prods, rows)]
        for (q, _, _), rw, bcat in zip(qkk, rows, bcats):
            qi_ref[rw, :] = _entry_factors(q, bcat, lbx[rw, :])
        for o, rw in zip(outs, rows):
            oi_ref[rw, :] = o

    def pair_body(pp, carry):
        do_group(2 * pp, 0)
        do_group(2 * pp + 1, 1)
        return carry

    lax.fori_loop(0, n_groups // 2, pair_body, 0)
    if n_groups % 2:
        do_group(n_groups - 1, 0)

    sf = jnp.zeros((k, k), F32)
    sb = jnp.zeros((k, k), F32)
    for j in range(nc):
        sf = d_ref[j][:, :k] * sf + u_ref[j][:, :k]
        jb = nc - 1 - j
        sb = d_ref[jb][:, k:] * sb + u_ref[jb][:, k:]

    def sweep_body(i, carry):
        sf, sb = carry
        s_ref[i, :, :k] = sf.astype(BF16)
        sf = d_ref[nc + i][:, :k] * sf + u_ref[nc + i][:, :k]
        j = nx - 1 - i
        s_ref[j, :, k:] = sb.astype(BF16)
        sb = d_ref[nc + j][:, k:] * sb + u_ref[nc + j][:, k:]
        return sf, sb

    lax.fori_loop(0, nx, sweep_body, (sf, sb))

    gain = gain_ref[...]

    def out_body(n, carry):
        rows = pl.ds(pl.multiple_of(n * c, c), c)
        o = oi_ref[rows, :] + pl.dot(qi_ref[rows, :], s_ref[n], trans_b=True)
        o = o * lax.rsqrt(jnp.mean(o * o, axis=-1, keepdims=True) + LN_EPS) * gain
        y_ref[rows, :] = (o * gx[rows, :].astype(F32)).astype(BF16)
        return carry

    lax.fori_loop(0, nx, out_body, 0, unroll=math.gcd(nx, 16))


def _hgrn2_scan(px, pc, gain, b, t, tc):
    qx, vx, kkx, lfx, gx = px
    qc, vc, kkc, lfc = pc
    k = HEAD_DIM
    heads = qx.shape[1] // k
    nx, nc = t // SCAN_BLOCK, tc // SCAN_BLOCK
    n_levels = SCAN_BLOCK.bit_length() - 1
    bits = (jnp.arange(SCAN_BLOCK)[None, :] >> jnp.arange(n_levels)[:, None]) & 1
    parity = jnp.broadcast_to(bits[:, :, None], (n_levels, SCAN_BLOCK, k)).astype(F32)
    fwd = lambda rows: pl.BlockSpec((rows, k), lambda i, h: (i, h))
    bwd = lambda rows: pl.BlockSpec((rows, k), lambda i, h: (i, h + heads))
    return pl.pallas_call(
        functools.partial(_scan_kernel, nx, nc),
        grid=(b, heads),
        in_specs=[fwd(t), fwd(t), fwd(t), bwd(t), fwd(t), bwd(t), fwd(t),
                  fwd(tc), fwd(tc), fwd(tc), bwd(tc), fwd(tc), bwd(tc),
                  pl.BlockSpec((1, k), lambda i, h: (0, 0)),
                  pl.BlockSpec((SCAN_BLOCK, SCAN_BLOCK), lambda i, h: (0, 0)),
                  pl.BlockSpec(parity.shape, lambda i, h: (0, 0, 0))],
        out_specs=fwd(t),
        out_shape=jax.ShapeDtypeStruct((b * t, heads * k), BF16),
        scratch_shapes=[pltpu.VMEM((t, k), F32),
                        pltpu.VMEM((t, 2 * k), BF16),
                        pltpu.VMEM((nx + nc, k, 2 * k), F32),
                        pltpu.VMEM((nx + nc, 1, 2 * k), F32),
                        pltpu.VMEM((nx, k, 2 * k), BF16),
                        pltpu.VMEM((2, math.gcd(nx, 4) * SCAN_BLOCK, 2 * k), F32)],
        compiler_params=_cparams("parallel", "parallel"),
        name="hgrn2_scan",
    )(qx, vx, kkx, kkx, lfx, lfx, gx, qc, vc, kkc, kkc, lfc, lfc, gain,
      jnp.tril(jnp.ones((SCAN_BLOCK, SCAN_BLOCK), BF16)), parity)


def _mix_kernel(n_groups, chunk, u_ref, v_ref, g_ref, b_ref, ws_ref, bias_ref, y_ref, vn_ref):
    gain, bias = g_ref[...], b_ref[...]
    for r0 in range(0, v_ref.shape[0], LN_ROWS):
        rows = pl.ds(r0, LN_ROWS)
        vn_ref[rows, :] = (_ln_rows(v_ref[rows, :].astype(F32)) * gain + bias).astype(BF16)
    gc = v_ref.shape[1] // n_groups
    for cidx in range(v_ref.shape[0] // chunk):
        rows = pl.ds(cidx * chunk, chunk)
        for g in range(n_groups):
            cols = pl.ds(g * gc, gc)
            mixed = jnp.dot(ws_ref[g], vn_ref[rows, cols], preferred_element_type=F32)
            y_ref[rows, cols] = (u_ref[rows, cols].astype(F32) * (mixed + bias_ref[:, cols])).astype(BF16)


def _chunk_mix(uv, v_g, v_b, w_s, bias, rows_per_step):
    m = uv.shape[0]
    bw = uv.shape[1] // 2
    groups, chunk, _ = w_s.shape
    tm = rows_per_step
    return pl.pallas_call(
        functools.partial(_mix_kernel, groups, chunk),
        grid=(m // tm,),
        in_specs=[pl.BlockSpec((tm, bw), lambda i: (i, 0)),
                  pl.BlockSpec((tm, bw), lambda i: (i, 1)),
                  pl.BlockSpec((1, bw), lambda i: (0, 0)),
                  pl.BlockSpec((1, bw), lambda i: (0, 0)),
                  pl.BlockSpec((groups, chunk, chunk), lambda i: (0, 0, 0)),
                  pl.BlockSpec((chunk, bw), lambda i: (0, 0))],
        out_specs=pl.BlockSpec((tm, bw), lambda i: (i, 0)),
        out_shape=jax.ShapeDtypeStruct((m, bw), BF16),
        scratch_shapes=[pltpu.VMEM((tm, bw), BF16)],
        compiler_params=_cparams("parallel"),
        name="chunk_mix",
    )(uv, uv, v_g, v_b, w_s, bias)


def _merge_kernel(n_cast, ya_ref, yb_ref, wa_ref, wb_ref, ga_ref, gb_ref, *refs):
    cast_src, o_ref, cast_dst = refs[:n_cast], refs[n_cast], refs[n_cast + 1:]
    a = jnp.dot(ya_ref[...], wa_ref[...], preferred_element_type=F32)
    bq = jnp.dot(yb_ref[...], wb_ref[...], preferred_element_type=F32)
    o_ref[...] = (ga_ref[...].astype(F32) * a + gb_ref[...].astype(F32) * bq).astype(BF16)
    _do_casts(cast_src, cast_dst)


def _merge(ya, yb, wa, wb, gates, casts=()):
    m, ka = ya.shape
    kb = yb.shape[1]
    d = wa.shape[1]
    tm, tn = min(MXU_TILE, m), min(MXU_TILE, d)
    nd = d // tn
    steps = (m // tm) * nd
    riding = _split_casts(casts, steps)
    c_in, c_out, c_shapes = _cast_specs(riding, steps, lambda i, j: i * nd + j)
    res = pl.pallas_call(
        functools.partial(_merge_kernel, len(riding)),
        grid=(m // tm, nd),
        in_specs=[pl.BlockSpec((tm, ka), lambda i, j: (i, 0)),
                  pl.BlockSpec((tm, kb), lambda i, j: (i, 0)),
                  pl.BlockSpec((ka, tn), lambda i, j: (0, j)),
                  pl.BlockSpec((kb, tn), lambda i, j: (0, j)),
                  pl.BlockSpec((tm, tn), lambda i, j: (i, j)),
                  pl.BlockSpec((tm, tn), lambda i, j: (i, j + nd))] + c_in,
        out_specs=[pl.BlockSpec((tm, tn), lambda i, j: (i, j))] + c_out,
        out_shape=[jax.ShapeDtypeStruct((m, d), BF16)] + c_shapes,
        compiler_params=_cparams("parallel", "arbitrary"),
        name="branch_merge",
    )(ya, yb, wa, wb, gates, gates, *[c.src for c in riding])
    return [res[0]] + _cast_results(casts, steps, res[1:])


def _resid_pos_kernel(alpha, n_row_code, x_ref, w_ref, r_ref, rt_ref, ct_ref, gate_ref, o_ref):
    y = jnp.dot(x_ref[...], w_ref[...], preferred_element_type=F32)
    is_row = (pl.program_id(1) < n_row_code).astype(F32)
    xp = r_ref[...] + is_row * rt_ref[...] + (1.0 - is_row) * ct_ref[...]
    o_ref[...] = alpha * xp.reshape(o_ref.shape) + gate_ref[...] * y


def _out_proj_resid(xin, w, x, row_tab, col_tab, mod3, gate_idx, alpha):
    m, kdim = xin.shape
    b, t, d = x.shape
    half = d // 2
    tm, tn = min(MXU_TILE, m, t), min(MXU_TILE, half)
    assert tm % GRID_W == 0 and half % tn == 0
    per_batch = t // tm
    nd, nh = d // tn, half // tn
    r = tm // GRID_W
    x3 = x.reshape(m // GRID_W, GRID_W, d)
    return pl.pallas_call(
        functools.partial(_resid_pos_kernel, alpha, nh),
        grid=(m // tm, nd),
        in_specs=[pl.BlockSpec((tm, kdim), lambda i, j: (i, 0)),
                  pl.BlockSpec((kdim, tn), lambda i, j: (0, j)),
                  pl.BlockSpec((r, GRID_W, tn), lambda i, j: (i, 0, j)),
                  pl.BlockSpec((r, 1, tn), lambda i, j: (i % per_batch, 0, jnp.minimum(j, nh - 1))),
                  pl.BlockSpec((1, GRID_W, tn), lambda i, j: (0, 0, jnp.maximum(j - nh, 0))),
                  pl.BlockSpec((None, 1, tn), lambda i, j: (i // per_batch, 0, gate_idx * nd + j))],
        out_specs=pl.BlockSpec((tm, tn), lambda i, j: (i, j)),
        out_shape=jax.ShapeDtypeStruct((m, d), F32),
        compiler_params=_cparams("parallel", "arbitrary", vmem=VMEM_LIMIT_WIDE),
        name="out_proj_resid",
    )(xin, w, x3, row_tab, col_tab, mod3)


def _ln_ffn_up_kernel(n_cast, p_ref, g_ref, b_ref, sh_ref, sc_ref, w_ref, *refs):
    cast_src, refs = refs[:n_cast], refs[n_cast:]
    x1_ref, act_ref = refs[0], refs[1]
    cast_dst, h_even, h_odd = refs[2:2 + n_cast], refs[2 + n_cast], refs[3 + n_cast]
    i, j = pl.program_id(0), pl.program_id(1)
    rows = p_ref.shape[0]

    def prepare(h_ref):
        g, b, sc, sh = g_ref[...], b_ref[...], 1.0 + sc_ref[...], sh_ref[...]
        for r0 in range(0, rows, LN_ROWS):
            x = _ln_rows(p_ref[pl.ds(r0, LN_ROWS), :]) * g + b
            x1_ref[pl.ds(r0, LN_ROWS), :] = x
            h = _ln_rows(x) * sc + sh
            h_ref[pl.ds(pl.multiple_of(j * rows + r0, LN_ROWS), LN_ROWS), :] = h.astype(BF16)
        _do_casts(cast_src, cast_dst)

    def multiply(h_ref):
        y = jnp.dot(h_ref[...], w_ref[...], preferred_element_type=F32)
        act_ref[...] = jnp.square(jnp.maximum(y, 0.0)).astype(BF16)

    @pl.when(i == 0)
    def _():
        prepare(h_even)

    @pl.when(i % 2 == 1)
    def _():
        multiply(h_even)
        prepare(h_odd)

    @pl.when(jnp.logical_and(i > 0, i % 2 == 0))
    def _():
        multiply(h_odd)
        prepare(h_even)


def _ln_ffn_up(pre, g, bvec, mod3, shift_idx, scale_idx, t, w, casts=()):
    m, d = pre.shape
    n = w.shape[1]
    tm, tn = min(MXU_TILE, m, t), min(MXU_TILE, n)
    n_i, n_j = m // tm, n // tn
    rows = tm // n_j
    assert rows * n_j == tm and rows % BF16_SUBLANES == 0
    per_batch = t // tm
    tile = lambda i: jnp.minimum(i, n_i - 1)
    prev = lambda i: jnp.maximum(i - 1, 0)
    steps = n_i * n_j
    slab = pl.BlockSpec((rows, d), lambda i, j: (jnp.minimum(i * n_j + j, steps - 1), 0))
    vec = pl.BlockSpec((1, d), lambda i, j: (0, 0))
    riding = _split_casts(casts, steps)
    c_in, c_out, c_shapes = _cast_specs(riding, steps, lambda i, j: jnp.where(i > 0, (i - 1) * n_j + j, 0))
    res = pl.pallas_call(
        functools.partial(_ln_ffn_up_kernel, len(riding)),
        grid=(n_i + 1, n_j),
        in_specs=[slab, vec, vec,
                  pl.BlockSpec((None, 1, d), lambda i, j: (tile(i) // per_batch, 0, shift_idx)),
                  pl.BlockSpec((None, 1, d), lambda i, j: (tile(i) // per_batch, 0, scale_idx)),
                  pl.BlockSpec((d, tn), lambda i, j: (0, jnp.where(i > 0, j, 0)))] + c_in,
        out_specs=[slab, pl.BlockSpec((tm, tn), lambda i, j: (prev(i), jnp.where(i > 0, j, 0)))] + c_out,
        out_shape=[jax.ShapeDtypeStruct((m, d), F32), jax.ShapeDtypeStruct((m, n), BF16)] + c_shapes,
        scratch_shapes=[pltpu.VMEM((tm, d), BF16), pltpu.VMEM((tm, d), BF16)],
        compiler_params=_cparams("arbitrary", "arbitrary"),
        name="ln_ffn_up",
    )(pre, g, bvec, mod3, mod3, w, *[c.src for c in riding])
    return [res[0], res[1]] + _cast_results(casts, steps, res[2:])


def _ln_kernel(p_ref, g_ref, b_ref, o_ref):
    g, b = g_ref[...], b_ref[...]

    def rows_body(i, c):
        rows = pl.ds(pl.multiple_of(i * LN_ROWS, LN_ROWS), LN_ROWS)
        o_ref[rows, :] = _ln_rows(p_ref[rows, :]) * g + b
        return c

    lax.fori_loop(0, p_ref.shape[0] // LN_ROWS, rows_body, 0, unroll=True)


def _ln(pre, g, bvec, tm):
    m, d = pre.shape
    row = pl.BlockSpec((tm, d), lambda i: (i, 0))
    vec = pl.BlockSpec((1, d), lambda i: (0, 0))
    return pl.pallas_call(
        _ln_kernel,
        grid=(m // tm,),
        in_specs=[row, vec, vec],
        out_specs=row,
        out_shape=jax.ShapeDtypeStruct((m, d), F32),
        compiler_params=_cparams("parallel"),
        name="ln_out",
    )(pre, g, bvec)


def _ffn2_kernel(alpha, n_k, a_ref, w_ref, r_ref, gate_ref, o_ref):
    kk = pl.program_id(2)
    last = n_k - 1
    part = lambda: jnp.dot(a_ref[...], w_ref[...], preferred_element_type=F32)
    if last == 0:
        o_ref[...] = alpha * r_ref[...] + gate_ref[...] * part()
        return

    @pl.when(kk == 0)
    def _():
        o_ref[...] = part()

    if last > 1:
        @pl.when(jnp.logical_and(kk > 0, kk < last))
        def _():
            o_ref[...] += part()

    @pl.when(kk == last)
    def _():
        o_ref[...] = alpha * r_ref[...] + gate_ref[...] * (o_ref[...] + part())


def _ffn2_resid(a, w, resid, mod3, gate_idx, alpha, t):
    m, f = a.shape
    d = w.shape[1]
    tm, tn, tk = min(MXU_TILE, m, t), min(MXU_TILE, d), min(4 * MXU_TILE, f)
    per_batch = t // tm
    nd = d // tn
    return pl.pallas_call(
        functools.partial(_ffn2_kernel, alpha, f // tk),
        grid=(m // tm, nd, f // tk),
        in_specs=[pl.BlockSpec((tm, tk), lambda i, j, kk: (i, kk)),
                  pl.BlockSpec((tk, tn), lambda i, j, kk: (kk, j)),
                  pl.BlockSpec((tm, tn), lambda i, j, kk: (i, j)),
                  pl.BlockSpec((None, 1, tn), lambda i, j, kk: (i // per_batch, 0, gate_idx * nd + j))],
        out_specs=pl.BlockSpec((tm, tn), lambda i, j, kk: (i, j)),
        out_shape=jax.ShapeDtypeStruct((m, d), F32),
        compiler_params=_cparams("parallel", "parallel", "arbitrary", vmem=VMEM_LIMIT_WIDE),
        name="ffn_down_resid",
    )(a, w, resid, mod3)


def _pos_tables(rows, cols, dim):
    quarter = dim // 4
    omega = 1.0 / (POS_BASE ** (jnp.arange(quarter, dtype=F32) / quarter))

    def emb(p):
        a = p[:, None] * omega[None, :]
        return jnp.concatenate([jnp.sin(a), jnp.cos(a)], axis=-1)

    return (emb(jnp.arange(rows, dtype=F32))[:, None, :], emb(jnp.arange(cols, dtype=F32))[None, :, :])


def kernel(x, c, ctx, c_ctx, w_ada, b_ada, w_in, lb_logits, a_norm_g, w_proj_a, v_norm_g, v_norm_b,
           w_s, b_s, w_proj_b, w_out, ln1_g, ln1_b, w_ff1, w_ff2, ln2_g, ln2_b):
    depth = w_ada.shape[0]
    assert depth == 1, "only the single-layer block is implemented"
    b, t, d = x.shape
    tc = ctx.shape[1]
    a_fdim = lb_logits.shape[-1]
    a_width = w_proj_a.shape[1]
    b_width = w_proj_b.shape[1]
    assert a_fdim == a_width and a_norm_g.shape[-1] == HEAD_DIM
    assert t % GRID_W == 0 and t % SCAN_BLOCK == 0 and tc % SCAN_BLOCK == 0
    assert w_s.shape[2] == SCAN_BLOCK or t % w_s.shape[2] == 0
    alpha = (2.0 * depth) ** 0.25
    m = b * t

    sizes = (a_fdim, a_width, a_fdim, a_fdim, a_width, b_width, b_width, d, d)
    offs = [0]
    for s in sizes:
        offs.append(offs[-1] + s)
    assert offs[-1] == w_in.shape[-1]

    w_in0 = w_in[0]
    seg = lambda first, last: _Cast(w_in0, offs[first], offs[last + 1] - offs[first])
    wq_b = w_in0[:, :offs[1]].astype(BF16)
    ws_b = w_s[0].astype(BF16)
    lb_all = jnp.cumsum(jax.nn.softmax(lb_logits.astype(F32), axis=1), axis=1)
    lb_row = jnp.concatenate([lb_all[0, 0], lb_all[1, 0]])[None, :]
    mix_bias = jnp.repeat(b_s[0].T, b_width // w_s.shape[1], axis=1)
    row_tab, col_tab = _pos_tables(t // GRID_W, GRID_W, d)

    n_rows = -(-(b + 1) // 8) * 8
    cc = jnp.concatenate([c, c_ctx[None, :], jnp.zeros((n_rows - b - 1, d), F32)], axis=0)
    mod = _modulation(cc, w_ada[0], b_ada[0][None, :])
    mod3 = mod.reshape(n_rows, 1, N_MOD * d)

    h = _adaln_pos(x, row_tab, col_tab, mod3, rows_per_step=min(8, t // GRID_W))
    hc = _adaln_ctx(ctx.reshape(b * tc, d), mod3, b, tm=min(256, b * tc))

    scale = HEAD_DIM ** -0.5
    ep_q = functools.partial(_ep_silu, scale)
    ep_g = functools.partial(_ep_silu, 1.0)

    qx, wi_b = _in_proj(h, wq_b, ep_q, [BF16], casts=(seg(1, 1),), name="in_proj_q")
    vx, wf_b = _in_proj(h, wi_b, _ep_cast, [BF16], casts=(seg(2, 3),), name="in_proj_i")
    kkx, lfx, wg_b = _in_proj(h, wf_b, _ep_decay, [BF16, F32], extra=(lb_row,), casts=(seg(4, 4),),
                              name="in_proj_f")
    gx, wuv_b = _in_proj(h, wg_b, ep_g, [BF16], casts=(seg(5, 6),), name="in_proj_g")
    uv, wgate_b = _in_proj(h, wuv_b, _ep_cast, [BF16], casts=(seg(7, 8),), name="in_proj_uv")
    gates, wa_b, wb_b, w1_b = _in_proj(
        h, wgate_b, _ep_sigmoid, [BF16],
        casts=(_Cast(w_proj_a[0], 0, d), _Cast(w_proj_b[0], 0, d), _Cast(w_ff1[0], 0, w_ff1.shape[-1])),
        name="in_proj_gates")

    (qc,) = _in_proj(hc, wq_b, ep_q, [BF16], name="in_proj_q_ctx")
    (vc,) = _in_proj(hc, wi_b, _ep_cast, [BF16], name="in_proj_i_ctx")
    kkc, lfc = _in_proj(hc, wf_b, _ep_decay, [BF16, F32], extra=(lb_row,), name="in_proj_f_ctx")

    ya = _hgrn2_scan((qx, vx, kkx, lfx, gx), (qc, vc, kkc, lfc), a_norm_g[0][None, :], b, t, tc)
    yb = _chunk_mix(uv, v_norm_g[0][None, :], v_norm_b[0][None, :], ws_b, mix_bias,
                    rows_per_step=min(4 * w_s.shape[2], t))

    merged, wo_b = _merge(ya, yb, wa_b, wb_b, gates, casts=(_Cast(w_out[0], 0, d),))
    pre1 = _out_proj_resid(merged, wo_b, x, row_tab, col_tab, mod3, 2, alpha)
    x1, act, w2_b = _ln_ffn_up(pre1, ln1_g[0][None, :], ln1_b[0][None, :], mod3, 3, 4, t, w1_b,
                               casts=(_Cast(w_ff2[0], 0, d),))
    pre2 = _ffn2_resid(act, w2_b, x1, mod3, 5, alpha, t)
    out = _ln(pre2, ln2_g[0][None, :], ln2_b[0][None, :], tm=min(512, m))
    return out.reshape(b, t, d)
```
